```python
import math
import jax, jax.numpy as jnp
from jax import lax
import numpy as np

D_MODEL = 1024
BATCH = 4
SEQ = 8192
DEPTH = 4

GRID_W = 64
CTX_LEN = 256

N_HEADS = 8
N_KV_HEADS = 2
HEAD_DIM = 64
ATTN_W = N_HEADS * HEAD_DIM
KV_W = N_KV_HEADS * HEAD_DIM
WINDOW = 128
BLOCK = 128
ROPE_THETA = 10000.0
AXIS_ROT = HEAD_DIM // 2

HY_CH = D_MODEL // 4
HY_ORDER = 2
HY_SHORT = 3
HY_EMB = 33
HY_BANDS = (HY_EMB - 1) // 2
HY_FILTER_W = 64
HY_TARGET = 1e-2
HY_FAST = 0.3
HY_SLOW = 1.5

CV_CH = D_MODEL // 4
CV_K = 31

K0 = ATTN_W
V0 = K0 + KV_W
HY0 = V0 + KV_W
CV0 = HY0 + (HY_ORDER + 1) * HY_CH
IN_W = CV0 + 2 * CV_CH
MIX_W = ATTN_W + HY_CH + CV_CH

D_FF = 2816
FFN_K = 3

EPS = 1e-6
NEG_INF = -1e30
F32 = jnp.float32

kernel_name = "hybrid_parallel_mixer_dit"


def rms_norm(t, g):
    t32 = t.astype(F32)
    y = t32 * lax.rsqrt(jnp.mean(t32 * t32, axis=-1, keepdims=True) + EPS) * g.astype(F32)
    return y.astype(t.dtype)


def layer_norm(t, g, b):
    t32 = t.astype(F32)
    mu = jnp.mean(t32, axis=-1, keepdims=True)
    var = jnp.mean(jnp.square(t32 - mu), axis=-1, keepdims=True)
    return ((t32 - mu) * lax.rsqrt(var + EPS) * g.astype(F32) + b.astype(F32)).astype(t.dtype)


def modulate(t, shift, scale):
    return t * (1 + scale) + shift


def dwconv(t, w, b):
    k, ch = w.shape
    pad = (k - 1) // 2
    y = lax.conv_general_dilated(t, w.astype(t.dtype)[:, None, :], window_strides=(1,),
                                 padding=[(pad, pad)], dimension_numbers=('NWC', 'WIO', 'NWC'),
                                 feature_group_count=ch)
    return y + b.astype(t.dtype)


def heads(t, n):
    b, l, _ = t.shape
    return t.reshape(b, l, n, HEAD_DIM)


def axial_rope_tables(rows):
    row = jnp.repeat(jnp.arange(rows, dtype=F32), GRID_W)
    col = jnp.tile(jnp.arange(GRID_W, dtype=F32), rows)
    inv = ROPE_THETA ** (-jnp.arange(0, AXIS_ROT, 2, dtype=F32) / AXIS_ROT)
    ang = jnp.stack([row[:, None] * inv, col[:, None] * inv], axis=1)
    return jnp.cos(ang), jnp.sin(ang)


def apply_rope(t, cos, sin):
    b, l, h, dh = t.shape
    t2 = t.astype(F32).reshape(b, l, h, 2, 2, AXIS_ROT // 2)
    a, bb = t2[..., 0, :], t2[..., 1, :]
    cs, sn = cos[None, :, None], sin[None, :, None]
    out = jnp.stack([a * cs - bb * sn, bb * cs + a * sn], axis=-2)
    return out.reshape(b, l, h, dh).astype(t.dtype)


def window_attention(q, k, v, k_ctx, v_ctx, sink):
    b, l, h, dh = q.shape
    nb = l // BLOCK
    g = h // N_KV_HEADS
    scale = HEAD_DIM ** -0.5
    qb = q.reshape(b, nb, BLOCK, N_KV_HEADS, g, dh)

    def band(t):
        tp = jnp.pad(t, ((0, 0), (BLOCK, BLOCK), (0, 0), (0, 0))).reshape(b, nb + 2, BLOCK, N_KV_HEADS, dh)
        return jnp.concatenate([tp[:, :-2], tp[:, 1:-1], tp[:, 2:]], axis=2)

    kb, vb = band(k), band(v)
    qi = jnp.arange(BLOCK)[:, None]
    kj = jnp.arange(3 * BLOCK)[None, :]
    kpos = (jnp.arange(nb)[:, None, None] - 1) * BLOCK + kj[None]
    mask = (jnp.abs(kj - BLOCK - qi) <= WINDOW)[None] & (kpos >= 0) & (kpos < l)

    s_loc = jnp.einsum('bnqkgd,bnckd->bnkgqc', qb, kb, preferred_element_type=F32) * scale
    s_loc = jnp.where(mask[None, :, None, None], s_loc, NEG_INF)
    s_ctx = jnp.einsum('bnqkgd,bckd->bnkgqc', qb, k_ctx, preferred_element_type=F32) * scale
    sink_b = sink.astype(F32).reshape(1, 1, N_KV_HEADS, g, 1, 1)
    m = jnp.maximum(jnp.maximum(s_loc.max(-1, keepdims=True), s_ctx.max(-1, keepdims=True)), sink_b)
    p_loc = jnp.exp(s_loc - m)
    p_ctx = jnp.exp(s_ctx - m)
    denom = p_loc.sum(-1) + p_ctx.sum(-1) + jnp.exp(sink_b - m)[..., 0]
    o = (jnp.einsum('bnkgqc,bnckd->bnqkgd', p_loc, vb.astype(F32))
         + jnp.einsum('bnkgqc,bckd->bnqkgd', p_ctx, v_ctx.astype(F32)))
    o = o / jnp.transpose(denom, (0, 1, 4, 2, 3))[..., None]
    return o.reshape(b, l, h * dh).astype(q.dtype)


def context_attention(q, k, v, sink):
    b, lc, h, dh = q.shape
    g = h // N_KV_HEADS
    qg = q.reshape(b, lc, N_KV_HEADS, g, dh)
    s = jnp.einsum('bqkgd,bckd->bkgqc', qg, k, preferred_element_type=F32) * (HEAD_DIM ** -0.5)
    sink_b = jnp.broadcast_to(sink.astype(F32).reshape(1, N_KV_HEADS, g, 1, 1), s.shape[:-1] + (1,))
    pr = jax.nn.softmax(jnp.concatenate([s, sink_b], axis=-1), axis=-1)[..., :-1]
    o = jnp.einsum('bkgqc,bckd->bqkgd', pr, v.astype(F32))
    return o.reshape(b, lc, h * dh).astype(q.dtype)


def hyena_filters(l, w1, b1, f1, w2, b2, f2, w3):
    t = jnp.linspace(0.0, 1.0, l, dtype=F32)[:, None]
    w = (2.0 * math.pi / l) * jnp.arange(l, dtype=F32)[:, None]
    f = jnp.linspace(1e-4, HY_BANDS - 1, HY_BANDS, dtype=F32)[None, :]
    z = jnp.concatenate([t, jnp.cos(f * w), -jnp.sin(f * w)], axis=-1)
    h = jnp.sin(f1.astype(F32) * (z @ w1.astype(F32) + b1.astype(F32)))
    h = jnp.sin(f2.astype(F32) * (h @ w2.astype(F32) + b2.astype(F32)))
    h = (h @ w3.astype(F32)).reshape(l, HY_ORDER, 2, HY_CH)
    deltas = jnp.abs(jnp.linspace(math.log(HY_TARGET) / HY_SLOW, math.log(HY_TARGET) / HY_FAST, HY_CH, dtype=F32))
    return h * jnp.exp(-t[:, :, None, None] * deltas)


def bidir_long_conv(u, h_fwd, h_bwd, bias):
    l = u.shape[1]
    n = 2 * l
    taps = jnp.concatenate([h_fwd, jnp.zeros_like(h_fwd[:1]), h_bwd[:0:-1]], axis=0)
    uf = jnp.fft.rfft(u.astype(F32), n=n, axis=1)
    tf = jnp.fft.rfft(taps, n=n, axis=0)
    y = jnp.fft.irfft(uf * tf[None], n=n, axis=1)[:, :l]
    return (y + u.astype(F32) * bias.astype(F32)).astype(u.dtype)


def hyena(u, short_w, short_b, w1, b1, f1, w2, b2, f2, w3, bias):
    l = u.shape[1]
    u = dwconv(u, short_w, short_b)
    x1, x2, z = jnp.split(u, 3, axis=-1)
    filt = hyena_filters(l, w1, b1, f1, w2, b2, f2, w3)
    for n, gate in enumerate((x1, x2)):
        z = gate * bidir_long_conv(z, filt[:, n, 0], filt[:, n, 1], bias[n])
    return z


def conformer_conv(u, dw_w, dw_b, ln_g, ln_b):
    a, gt = jnp.split(u, 2, axis=-1)
    h = dwconv(a * jax.nn.sigmoid(gt), dw_w, dw_b)
    return jax.nn.silu(layer_norm(h, ln_g, ln_b))


def ffn_sublayer(t, shift, scale, gate, g_pre, g_post, w_up, dw_w, dw_b, w_down):
    h = modulate(rms_norm(t, g_pre), shift, scale)
    u, gt = jnp.split(h @ w_up, 2, axis=-1)
    y = (jax.nn.silu(dwconv(gt, dw_w, dw_b)) * u) @ w_down
    return t + gate * rms_norm(y, g_post)


def setup_inputs(seed: int = 0) -> dict:
    key = jax.random.key(seed)
    ks = iter(jax.random.split(key, 40))

    def nrm(shape, scale):
        return jax.random.normal(next(ks), shape, F32) * scale

    def gain(shape):
        return 1.0 + nrm(shape, 0.02)

    L = DEPTH
    D = D_MODEL
    return {
        "x": nrm((BATCH, SEQ, D), 1.0),
        "c": nrm((BATCH, D), 1.0),
        "ctx": nrm((BATCH, CTX_LEN, D), 1.0),
        "c_ctx": nrm((D,), 1.0),
        "w_mod": nrm((L, D, 6 * D), 0.5 * D ** -0.5),
        "b_mod": nrm((L, 6 * D), 0.02),
        "g_pre_mix": gain((L, D)),
        "g_post_mix": gain((L, D)),
        "g_pre_ffn": gain((L, D)),
        "g_post_ffn": gain((L, D)),
        "w_in": nrm((L, D, IN_W), D ** -0.5),
        "attn_sink": nrm((L, N_HEADS), 0.5),
        "hy_short_w": nrm((L, HY_SHORT, (HY_ORDER + 1) * HY_CH), HY_SHORT ** -0.5),
        "hy_short_b": nrm((L, (HY_ORDER + 1) * HY_CH), 0.02),
        "hy_w1": nrm((L, HY_EMB, HY_FILTER_W), HY_EMB ** -0.5),
        "hy_b1": nrm((L, HY_FILTER_W), 0.1),
        "hy_freq1": gain((L, HY_FILTER_W)),
        "hy_w2": nrm((L, HY_FILTER_W, HY_FILTER_W), HY_FILTER_W ** -0.5),
        "hy_b2": nrm((L, HY_FILTER_W), 0.1),
        "hy_freq2": gain((L, HY_FILTER_W)),
        "hy_w3": nrm((L, HY_FILTER_W, HY_ORDER * 2 * HY_CH), 0.03 * HY_FILTER_W ** -0.5),
        "hy_bias": nrm((L, HY_ORDER, HY_CH), 0.5),
        "cv_dw_w": nrm((L, CV_K, CV_CH), CV_K ** -0.5),
        "cv_dw_b": nrm((L, CV_CH), 0.02),
        "cv_ln_g": gain((L, CV_CH)),
        "cv_ln_b": nrm((L, CV_CH), 0.02),
        "w_out": nrm((L, MIX_W, D), MIX_W ** -0.5),
        "w_up": nrm((L, D, 2 * D_FF), D ** -0.5),
        "ffn_dw_w": nrm((L, FFN_K, D_FF), FFN_K ** -0.5),
        "ffn_dw_b": nrm((L, D_FF), 0.02),
        "w_down": nrm((L, D_FF, D), D_FF ** -0.5),
    }


def reference(x, c, ctx, c_ctx, w_mod, b_mod, g_pre_mix, g_post_mix, g_pre_ffn, g_post_ffn,
              w_in, attn_sink, hy_short_w, hy_short_b, hy_w1, hy_b1, hy_freq1, hy_w2, hy_b2, hy_freq2,
              hy_w3, hy_bias, cv_dw_w, cv_dw_b, cv_ln_g, cv_ln_b, w_out, w_up, ffn_dw_w, ffn_dw_b, w_down):
    b, l, d = x.shape
    rows = l // GRID_W
    cos, sin = axial_rope_tables(rows)
    s_c = jax.nn.silu(c)
    s_cc = jax.nn.silu(c_ctx)

    for i in range(DEPTH):
        last = i == DEPTH - 1
        hy_p = (hy_short_w[i], hy_short_b[i], hy_w1[i], hy_b1[i], hy_freq1[i], hy_w2[i], hy_b2[i],
                hy_freq2[i], hy_w3[i], hy_bias[i])
        cv_p = (cv_dw_w[i], cv_dw_b[i], cv_ln_g[i], cv_ln_b[i])

        mod = s_c @ w_mod[i] + b_mod[i]
        sh_a, sc_a, g_a, sh_f, sc_f, g_f = [m[:, None, :] for m in jnp.split(mod, 6, axis=-1)]
        n_ctx_mod = 2 if last else 6
        mods_c = jnp.split(s_cc @ w_mod[i][:, :n_ctx_mod * d] + b_mod[i][:n_ctx_mod * d], n_ctx_mod)

        h = modulate(rms_norm(x, g_pre_mix[i]), sh_a, sc_a)
        hc = modulate(rms_norm(ctx, g_pre_mix[i]), mods_c[0], mods_c[1])
        p = h @ w_in[i]
        pc = hc @ (w_in[i][:, K0:HY0] if last else w_in[i])
        pc_kv = pc if last else pc[..., K0:HY0]
        k_ctx = heads(pc_kv[..., :KV_W], N_KV_HEADS)
        v_ctx = heads(pc_kv[..., KV_W:], N_KV_HEADS)

        q = apply_rope(heads(p[..., :K0], N_HEADS), cos, sin)
        k = apply_rope(heads(p[..., K0:V0], N_KV_HEADS), cos, sin)
        v = heads(p[..., V0:HY0], N_KV_HEADS)
        attn = window_attention(q, k, v, k_ctx, v_ctx, attn_sink[i])
        hy = hyena(p[..., HY0:CV0], *hy_p)
        cv = conformer_conv(p[..., CV0:], *cv_p)
        y = jnp.concatenate([attn, hy, cv], axis=-1) @ w_out[i]
        x = x + g_a * rms_norm(y, g_post_mix[i])

        x = ffn_sublayer(x, sh_f, sc_f, g_f, g_pre_ffn[i], g_post_ffn[i], w_up[i], ffn_dw_w[i], ffn_dw_b[i], w_down[i])

        if not last:
            qc = heads(pc[..., :K0], N_HEADS)
            attn_c = context_attention(qc, k_ctx, v_ctx, attn_sink[i])
            hy_c = hyena(pc[..., HY0:CV0], *hy_p)
            cv_c = conformer_conv(pc[..., CV0:], *cv_p)
            yc = jnp.concatenate([attn_c, hy_c, cv_c], axis=-1) @ w_out[i]
            ctx = ctx + mods_c[2] * rms_norm(yc, g_post_mix[i])
            ctx = ffn_sublayer(ctx, mods_c[3], mods_c[4], mods_c[5], g_pre_ffn[i], g_post_ffn[i],
                               w_up[i], ffn_dw_w[i], ffn_dw_b[i], w_down[i])
    return x
```

```python
import functools
import math

import numpy as np
import jax
import jax.numpy as jnp
from jax import lax
from jax.experimental import pallas as pl
from jax.experimental.pallas import tpu as pltpu

F32 = jnp.float32
BF16 = jnp.bfloat16
HIGHEST = lax.Precision.HIGHEST

N_HEADS = 8
N_KV_HEADS = 2
HEAD_DIM = 64
GROUP = N_HEADS // N_KV_HEADS
ATTN_W = N_HEADS * HEAD_DIM
KV_W = N_KV_HEADS * HEAD_DIM
WINDOW = 128
GRID_W = 64
ROPE_THETA = 10000.0
AXIS_ROT = HEAD_DIM // 2
HY_ORDER = 2
HY_EMB = 33
HY_BANDS = (HY_EMB - 1) // 2
HY_TARGET = 1e-2
HY_FAST = 0.3
HY_SLOW = 1.5
CV_K = 31
FFN_K = 3
EPS = 1e-6
NEG_INF = -1e30

LANES = 128
HALO = 16
VMEM_LIMIT = 56 * 1024 * 1024

NT = (((1,), (1,)), ((), ()))


def _cparams(n_axes):
    return pltpu.CompilerParams(dimension_semantics=("parallel",) * n_axes,
                                vmem_limit_bytes=VMEM_LIMIT)


def _full_spec(shape):
    nd = len(shape)
    return pl.BlockSpec(shape, lambda *_: (0,) * nd)


def _layer_spec(shape, layer):
    nd = len(shape)
    return pl.BlockSpec((None,) + tuple(shape), lambda *_: (layer,) + (0,) * nd)


def _rms(t, g):
    return t * lax.rsqrt(jnp.mean(t * t, axis=-1, keepdims=True) + EPS) * g


def _silu(t):
    return t * jax.nn.sigmoid(t)


def _mod_kernel(c_ref, w_ref, b_ref, o_ref):
    s = _silu(c_ref[...])
    o_ref[0] = jnp.dot(s, w_ref[0], preferred_element_type=F32, precision=HIGHEST) + b_ref[0]


def _modulation(cond, w_mod, b_mod):
    depth, d, n = w_mod.shape
    rows = cond.shape[0]
    tn = 1536
    assert n % tn == 0
    return pl.pallas_call(
        _mod_kernel,
        grid=(depth, n // tn),
        in_specs=[pl.BlockSpec((rows, d), lambda i, j: (0, 0)),
                  pl.BlockSpec((1, d, tn), lambda i, j: (i, 0, j)),
                  pl.BlockSpec((1, 1, tn), lambda i, j: (i, 0, j))],
        out_specs=pl.BlockSpec((1, rows, tn), lambda i, j: (i, 0, j)),
        out_shape=jax.ShapeDtypeStruct((depth, rows, n), F32),
        compiler_params=_cparams(2),
        name="modulation",
    )(cond, w_mod, b_mod.reshape(depth, 1, n))


def _swap_halves(t):
    w = t.shape[-1]
    lane = lax.broadcasted_iota(jnp.int32, t.shape, 1)
    from_hi = pltpu.roll(t, w - AXIS_ROT // 2, axis=1)
    from_lo = pltpu.roll(t, AXIS_ROT // 2, axis=1)
    return jnp.where(lane % AXIS_ROT < AXIS_ROT // 2, from_hi, from_lo)


def _inproj_kernel(rope, x_ref, m_ref, g_ref, wqkv_ref, whyt_ref, wcv_ref, cs_ref, sn_ref,
                   q_ref, k_ref, v_ref, hy_ref, cv_ref):
    x = x_ref[0]
    h = _rms(x, g_ref[...]) * (1.0 + m_ref[0, 1:2, :]) + m_ref[0, 0:1, :]
    hb = h.astype(BF16)
    p = jnp.dot(hb, wqkv_ref[...], preferred_element_type=F32)
    q = p[:, :ATTN_W]
    k = p[:, ATTN_W:ATTN_W + KV_W]
    if rope:
        cs = cs_ref[...]
        sn = sn_ref[...]
        cq = jnp.concatenate([cs] * (ATTN_W // LANES), axis=1)
        sq = jnp.concatenate([sn] * (ATTN_W // LANES), axis=1)
        q = q * cq + _swap_halves(q) * sq
        k = k * cs + _swap_halves(k) * sn
    q_ref[0] = (q * (HEAD_DIM ** -0.5)).astype(BF16)
    k_ref[0] = k.astype(BF16)
    v_ref[0] = p[:, ATTN_W + KV_W:].astype(BF16)
    hy_ref[0] = lax.dot_general(whyt_ref[...], hb, NT, preferred_element_type=F32).astype(BF16)
    cv_ref[0] = jnp.dot(hb, wcv_ref[...], preferred_element_type=F32).astype(BF16)


def _inproj(x, mods, mod_row, g_pre, wqkv, whyt, wcv, layer, cs, sn, tm):
    b, l, d = x.shape
    rope = cs is not None
    if not rope:
        cs = jnp.zeros((l, LANES), F32)
        sn = cs
    nhy = whyt.shape[1]
    ncv = wcv.shape[2]
    mrow = (lambda bi: bi) if mod_row is None else (lambda bi: mod_row)
    return pl.pallas_call(
        functools.partial(_inproj_kernel, rope),
        grid=(b, l // tm),
        in_specs=[pl.BlockSpec((1, tm, d), lambda bi, j: (bi, j, 0)),
                  pl.BlockSpec((1, 6, d), lambda bi, j: (mrow(bi), 0, 0)),
                  _layer_spec((1, d), layer),
                  _layer_spec(wqkv.shape[1:], layer),
                  _layer_spec(whyt.shape[1:], layer),
                  _layer_spec(wcv.shape[1:], layer),
                  pl.BlockSpec((tm, LANES), lambda bi, j: (j, 0)),
                  pl.BlockSpec((tm, LANES), lambda bi, j: (j, 0))],
        out_specs=[pl.BlockSpec((1, tm, ATTN_W), lambda bi, j: (bi, j, 0)),
                   pl.BlockSpec((1, tm, KV_W), lambda bi, j: (bi, j, 0)),
                   pl.BlockSpec((1, tm, KV_W), lambda bi, j: (bi, j, 0)),
                   pl.BlockSpec((1, nhy, tm), lambda bi, j: (bi, 0, j)),
                   pl.BlockSpec((1, tm, ncv), lambda bi, j: (bi, j, 0))],
        out_shape=[jax.ShapeDtypeStruct((b, l, ATTN_W), BF16),
                   jax.ShapeDtypeStruct((b, l, KV_W), BF16),
                   jax.ShapeDtypeStruct((b, l, KV_W), BF16),
                   jax.ShapeDtypeStruct((b, nhy, l), BF16),
                   jax.ShapeDtypeStruct((b, l, ncv), BF16)],
        compiler_params=_cparams(2),
        name="inproj",
    )(x, mods, g_pre, wqkv, whyt, wcv, cs, sn)


def _attend(q, kk, vv, sink_ref, mask, o_ref):
    for h in range(N_HEADS):
        kv = h // GROUP
        qh = q[:, h * HEAD_DIM:(h + 1) * HEAD_DIM]
        kh = kk[:, kv * HEAD_DIM:(kv + 1) * HEAD_DIM]
        vh = vv[:, kv * HEAD_DIM:(kv + 1) * HEAD_DIM]
        s = lax.dot_general(qh, kh, NT, preferred_element_type=F32)
        if mask is not None:
            s = jnp.where(mask, s, NEG_INF)
        sink = sink_ref[0, h]
        m = jnp.maximum(jnp.max(s, axis=-1, keepdims=True), sink)
        p = jnp.exp(s - m)
        denom = jnp.sum(p, axis=-1, keepdims=True) + jnp.exp(sink - m)
        o = jnp.dot(p.astype(BF16), vh, preferred_element_type=F32) / denom
        o_ref[0, :, h * HEAD_DIM:(h + 1) * HEAD_DIM] = o.astype(o_ref.dtype)


def _win_attn_kernel(seq_len, sink_ref, q_ref, kp_ref, kc_ref, kn_ref, vp_ref, vc_ref, vn_ref,
                     kx_ref, vx_ref, o_ref):
    tq = q_ref.shape[1]
    n_ctx = kx_ref.shape[1]
    j = pl.program_id(1)
    kk = jnp.concatenate([kp_ref[0], kc_ref[0], kn_ref[0], kx_ref[0]], axis=0)
    vv = jnp.concatenate([vp_ref[0], vc_ref[0], vn_ref[0], vx_ref[0]], axis=0)
    nk = tq + 2 * WINDOW + n_ctx
    r = lax.broadcasted_iota(jnp.int32, (tq, nk), 0)
    c = lax.broadcasted_iota(jnp.int32, (tq, nk), 1)
    kpos = j * tq - WINDOW + c
    local_ok = (c >= r) & (c <= r + 2 * WINDOW) & (kpos >= 0) & (kpos < seq_len)
    mask = local_ok | (c >= tq + 2 * WINDOW)
    _attend(q_ref[0], kk, vv, sink_ref, mask, o_ref)


def _window_attention(q, k, v, k_ctx, v_ctx, sink, tq):
    b, l, _ = q.shape
    n_ctx = k_ctx.shape[1]
    assert tq % WINDOW == 0 and l % tq == 0
    r = tq // WINDOW
    nwb = l // WINDOW
    prev = pl.BlockSpec((1, WINDOW, KV_W), lambda bi, j: (bi, jnp.maximum(j * r - 1, 0), 0))
    cur = pl.BlockSpec((1, tq, KV_W), lambda bi, j: (bi, j, 0))
    nxt = pl.BlockSpec((1, WINDOW, KV_W), lambda bi, j: (bi, jnp.minimum((j + 1) * r, nwb - 1), 0))
    ctx = pl.BlockSpec((1, n_ctx, KV_W), lambda bi, j: (bi, 0, 0))
    return pl.pallas_call(
        functools.partial(_win_attn_kernel, l),
        grid=(b, l // tq),
        in_specs=[pl.BlockSpec(memory_space=pltpu.SMEM),
                  pl.BlockSpec((1, tq, ATTN_W), lambda bi, j: (bi, j, 0)),
                  prev, cur, nxt, prev, cur, nxt, ctx, ctx],
        out_specs=pl.BlockSpec((1, tq, ATTN_W), lambda bi, j: (bi, j, 0)),
        out_shape=jax.ShapeDtypeStruct((b, l, ATTN_W), BF16),
        compiler_params=_cparams(2),
        name="window_attention",
    )(sink, q, k, k, k, v, v, v, k_ctx, v_ctx)


def _ctx_attn_kernel(sink_ref, q_ref, k_ref, v_ref, o_ref):
    _attend(q_ref[0], k_ref[0], v_ref[0], sink_ref, None, o_ref)


def _context_attention(q, k, v, sink):
    b, l, _ = q.shape
    return pl.pallas_call(
        _ctx_attn_kernel,
        grid=(b,),
        in_specs=[pl.BlockSpec(memory_space=pltpu.SMEM),
                  pl.BlockSpec((1, l, ATTN_W), lambda bi: (bi, 0, 0)),
                  pl.BlockSpec((1, l, KV_W), lambda bi: (bi, 0, 0)),
                  pl.BlockSpec((1, l, KV_W), lambda bi: (bi, 0, 0))],
        out_specs=pl.BlockSpec((1, l, ATTN_W), lambda bi: (bi, 0, 0)),
        out_shape=jax.ShapeDtypeStruct((b, l, ATTN_W), BF16),
        compiler_params=_cparams(1),
        name="context_attention",
    )(sink, q, k, v)


def _taps_kernel(seq_len, z_ref, t_ref, w1_ref, b1_ref, f1_ref, w2_ref, b2_ref, f2_ref, w3t_ref,
                 dl_ref, o_ref):
    tn = z_ref.shape[0]
    ch = dl_ref.shape[0]
    h = jnp.sin(f1_ref[...] * (jnp.dot(z_ref[...], w1_ref[...], preferred_element_type=F32,
                                       precision=HIGHEST) + b1_ref[...]))
    h = jnp.sin(f2_ref[...] * (jnp.dot(h, w2_ref[...], preferred_element_type=F32,
                                       precision=HIGHEST) + b2_ref[...]))
    full = lax.dot_general(w3t_ref[...], h, NT, preferred_element_type=F32, precision=HIGHEST)
    n0 = pl.program_id(0) * tn
    is_fwd = n0 < seq_len
    decay = jnp.exp(-t_ref[...] * dl_ref[...])
    pos = n0 + lax.broadcasted_iota(jnp.int32, (1, tn), 1)
    for o in range(HY_ORDER):
        fwd = full[o * 2 * ch:o * 2 * ch + ch]
        bwd = full[o * 2 * ch + ch:(o + 1) * 2 * ch]
        taps = jnp.where(is_fwd, fwd, bwd) * decay
        o_ref[o] = jnp.where(pos == seq_len, 0.0, taps)


def _filter_tables(l):
    t = jnp.linspace(0.0, 1.0, l, dtype=F32)[:, None]
    w = (2.0 * math.pi / l) * jnp.arange(l, dtype=F32)[:, None]
    f = jnp.linspace(1e-4, HY_BANDS - 1, HY_BANDS, dtype=F32)[None, :]
    z = jnp.concatenate([t, jnp.cos(f * w), -jnp.sin(f * w)], axis=-1)
    n = np.arange(2 * l)
    idx = np.minimum(np.where(n < l, n, 2 * l - n), l - 1)
    zc = jnp.pad(z[idx], ((0, 0), (0, LANES - HY_EMB)))
    return zc, zc[:, 0][None, :]


def _hyena_taps(l, tables, w1p, b1, f1, w2, b2, f2, w3t, deltas, layer):
    zc, trow = tables
    ch = deltas.shape[0]
    fw = w2.shape[1]
    tn = min(1024, l)
    return pl.pallas_call(
        functools.partial(_taps_kernel, l),
        grid=(2 * l // tn,),
        in_specs=[pl.BlockSpec((tn, LANES), lambda j: (j, 0)),
                  pl.BlockSpec((1, tn), lambda j: (0, j)),
                  _layer_spec((LANES, fw), layer),
                  _layer_spec((1, fw), layer),
                  _layer_spec((1, fw), layer),
                  _layer_spec((fw, fw), layer),
                  _layer_spec((1, fw), layer),
                  _layer_spec((1, fw), layer),
                  _layer_spec(w3t.shape[1:], layer),
                  _full_spec((ch, 1))],
        out_specs=pl.BlockSpec((HY_ORDER, ch, tn), lambda j: (0, 0, j)),
        out_shape=jax.ShapeDtypeStruct((HY_ORDER, ch, 2 * l), F32),
        compiler_params=_cparams(1),
        name="hyena_taps",
    )(zc, trow, w1p, b1, f1, w2, b2, f2, w3t, deltas)


def _dft_consts(na, nb):
    a = np.arange(na)
    fa = np.exp(-2j * np.pi * np.outer(a, a) / na)
    b = np.arange(nb)
    g = np.exp(-2j * np.pi * np.outer(b, b) / nb)
    fr, fi = fa.real, fa.imag
    gr, gi = g.real, g.imag
    h = na // 2
    m1 = np.block([[fr[:, :h], -fi[:, :h]], [fi[:, :h], fr[:, :h]]])
    m1f = np.concatenate([fr, fi], axis=0)
    m2 = np.block([[gr, gi], [-gi, gr]])
    m2i = np.block([[gr, -gi], [gi, gr]])
    m1i = np.block([[fr[:h], fi[:h]], [-fi[:h], fr[:h]]])
    tw = np.exp(-2j * np.pi * np.outer(a, b) / (na * nb))
    bf = lambda m: jnp.asarray(m, dtype=F32).astype(BF16)
    return dict(m1=bf(m1), m1f=bf(m1f), m2=bf(m2), m2i=bf(m2i), m1i=bf(m1i),
                tr=jnp.asarray(tw.real, F32), ti=jnp.asarray(tw.imag, F32))


def _stage2(a_mat, tr, ti, m2, na, nb):
    ar, ai = a_mat[:na], a_mat[na:]
    tr2 = jnp.concatenate([tr, tr], axis=1)
    ti2 = jnp.concatenate([ti, ti], axis=1)
    pr = ar * tr2 - ai * ti2
    pi = ar * ti2 + ai * tr2
    lhs = jnp.concatenate([jnp.concatenate([pr[:, :nb], pi[:, :nb]], axis=1),
                           jnp.concatenate([pr[:, nb:], pi[:, nb:]], axis=1)], axis=0)
    return jnp.dot(lhs.astype(BF16), m2, preferred_element_type=F32)


def _spectrum_kernel(scale, t_ref, m1f_ref, m2_ref, tr_ref, ti_ref, o_ref):
    na, nb = tr_ref.shape
    tp = jnp.concatenate([t_ref[0, 0], t_ref[0, 1]], axis=1).astype(BF16)
    a_mat = jnp.dot(m1f_ref[...], tp, preferred_element_type=F32)
    x = _stage2(a_mat, tr_ref[...], ti_ref[...], m2_ref[...], na, nb) * scale
    o_ref[0, 0] = x[:na]
    o_ref[0, 1] = x[na:]


def _filter_spectrum(taps, dc):
    o, ch, n = taps.shape
    nb = LANES
    na = n // nb
    t4 = taps.reshape(o, ch, na, nb)
    return pl.pallas_call(
        functools.partial(_spectrum_kernel, 1.0 / n),
        grid=(o, ch // 2),
        in_specs=[pl.BlockSpec((1, 2, na, nb), lambda i, j: (i, j, 0, 0)),
                  _full_spec((2 * na, na)), _full_spec((2 * nb, 2 * nb)),
                  _full_spec((na, nb)), _full_spec((na, nb))],
        out_specs=pl.BlockSpec((1, 2, na, 2 * nb), lambda i, j: (i, j, 0, 0)),
        out_shape=jax.ShapeDtypeStruct((o, ch, na, 2 * nb), F32),
        compiler_params=_cparams(2),
        name="hyena_spectrum",
    )(t4, dc["m1f"], dc["m2"], dc["tr"], dc["ti"])


def _shift_prev(x):
    rows, nb = x.shape
    lane = lax.broadcasted_iota(jnp.int32, x.shape, 1)
    row = lax.broadcasted_iota(jnp.int32, x.shape, 0)
    r = pltpu.roll(x, 1, axis=1)
    r2 = pltpu.roll(r, 1, axis=0)
    return jnp.where(lane == 0, jnp.where(row == 0, 0.0, r2), r)


def _shift_next(x):
    rows, nb = x.shape
    lane = lax.broadcasted_iota(jnp.int32, x.shape, 1)
    row = lax.broadcasted_iota(jnp.int32, x.shape, 0)
    r = pltpu.roll(x, nb - 1, axis=1)
    r2 = pltpu.roll(r, rows - 1, axis=0)
    return jnp.where(lane == nb - 1, jnp.where(row == rows - 1, 0.0, r2), r)


def _fft_conv(z, hf, m1, m2, m2i, m1i, tr, ti):
    na, nb = tr.shape
    zc = jnp.concatenate([jnp.concatenate([z[0][0], z[0][1]], axis=0),
                          jnp.concatenate([z[1][0], z[1][1]], axis=0)], axis=1)
    a_mat = jnp.dot(m1, zc.astype(BF16), preferred_element_type=F32)
    x = _stage2(a_mat, tr, ti, m2, na, nb)
    hh = jnp.concatenate([hf[0], hf[1]], axis=0)
    xr, xi = x[:, :nb], x[:, nb:]
    hr, hi = hh[:, :nb], hh[:, nb:]
    y = jnp.concatenate([xr * hr - xi * hi, xr * hi + xi * hr], axis=1)
    bm = jnp.dot(y.astype(BF16), m2i, preferred_element_type=F32)
    br, bi = bm[:, :nb], bm[:, nb:]
    trs = jnp.concatenate([tr, tr], axis=0)
    tis = jnp.concatenate([ti, ti], axis=0)
    pr = br * trs + bi * tis
    pi = bi * trs - br * tis
    rhs = jnp.concatenate([jnp.concatenate([pr[:na], pr[na:]], axis=1),
                           jnp.concatenate([pi[:na], pi[na:]], axis=1)], axis=0)
    yo = jnp.dot(m1i, rhs.astype(BF16), preferred_element_type=F32)
    h = na // 2
    return [[yo[:h, :nb], yo[h:, :nb]], [yo[:h, nb:], yo[h:, nb:]]]


def _hyena_kernel(sw_ref, sb_ref, hb_ref, x1_ref, x2_ref, z_ref, hf_ref,
                  m1_ref, m2_ref, m2i_ref, m1i_ref, tr_ref, ti_ref, o_ref):
    nbatch = x1_ref.shape[0]
    nch = hb_ref.shape[1]
    c0 = pl.program_id(0) * 2
    mats = (m1_ref[...], m2_ref[...], m2i_ref[...], m1i_ref[...], tr_ref[...], ti_ref[...])

    def short(ref, stream, b, cl):
        ch = stream * nch + c0 + cl
        x = ref[b, cl].astype(F32)
        return (sw_ref[0, ch] * _shift_prev(x) + sw_ref[1, ch] * x
                + sw_ref[2, ch] * _shift_next(x) + sb_ref[0, ch])

    for pair in range(nbatch // 2):
        bs = (2 * pair, 2 * pair + 1)
        zz = [[short(z_ref, 2, b, cl) for b in bs] for cl in range(2)]
        gates = (x1_ref, x2_ref)
        for o in range(HY_ORDER):
            y = _fft_conv(zz, [hf_ref[o, 0], hf_ref[o, 1]], *mats)
            zz = [[short(gates[o], o, bs[p], cl) * (y[cl][p] + hb_ref[o, c0 + cl] * zz[cl][p])
                   for p in range(2)] for cl in range(2)]
        for cl in range(2):
            for p in range(2):
                o_ref[bs[p], cl] = zz[cl][p].astype(o_ref.dtype)


def _hyena_long(hyt, hf, short_w, short_b, hy_bias, dc):
    b, c3, l = hyt.shape
    ch = c3 // 3
    nb = LANES
    na = 2 * l // nb
    assert b % 2 == 0 and ch % 2 == 0
    h4 = hyt.reshape(b, c3, na // 2, nb)
    blk = (b, 2, na // 2, nb)
    smem = pl.BlockSpec(memory_space=pltpu.SMEM)
    out = pl.pallas_call(
        _hyena_kernel,
        grid=(ch // 2,),
        in_specs=[smem, smem, smem,
                  pl.BlockSpec(blk, lambda j: (0, j, 0, 0)),
                  pl.BlockSpec(blk, lambda j: (0, ch // 2 + j, 0, 0)),
                  pl.BlockSpec(blk, lambda j: (0, ch + j, 0, 0)),
                  pl.BlockSpec((HY_ORDER, 2, na, 2 * nb), lambda j: (0, j, 0, 0)),
                  _full_spec((2 * na, na)), _full_spec((2 * nb, 2 * nb)),
                  _full_spec((2 * nb, 2 * nb)), _full_spec((na, 2 * na)),
                  _full_spec((na, nb)), _full_spec((na, nb))],
        out_specs=pl.BlockSpec(blk, lambda j: (0, j, 0, 0)),
        out_shape=jax.ShapeDtypeStruct((b, ch, na // 2, nb), BF16),
        compiler_params=_cparams(1),
        name="hyena_long",
    )(short_w, short_b, hy_bias, h4, h4, h4, hf,
      dc["m1"], dc["m2"], dc["m2i"], dc["m1i"], dc["tr"], dc["ti"])
    return out.reshape(b, ch, l)


def _dense_dft_consts(l):
    n = 2 * l
    k = np.arange(n)
    ang = 2 * np.pi * np.outer(k, k) / n
    cos, sin = np.cos(ang), np.sin(ang)
    fwd = np.concatenate([cos, -sin], axis=1)
    inv = np.concatenate([cos[:, :l], -sin[:, :l]], axis=0) / n
    bf = lambda m: jnp.asarray(m, dtype=F32).astype(BF16)
    return dict(fwd_half=bf(fwd[:l]), fwd=bf(fwd), inv=bf(inv))


def _hyena_short_kernel(sw_ref, sb_ref, hb_ref, x1_ref, x2_ref, z_ref, taps_ref,
                        fh_ref, ff_ref, inv_ref, o_ref):
    nbatch, ct, l = x1_ref.shape
    n = 2 * l
    lane = lax.broadcasted_iota(jnp.int32, (ct, l), 1)

    def short(ref, stream, b):
        x = ref[b].astype(F32)
        prev = jnp.where(lane == 0, 0.0, pltpu.roll(x, 1, axis=1))
        nxt = jnp.where(lane == l - 1, 0.0, pltpu.roll(x, l - 1, axis=1))
        return sw_ref[stream, 0] * prev + sw_ref[stream, 1] * x + sw_ref[stream, 2] * nxt + sb_ref[stream]

    gates = (x1_ref, x2_ref)
    zz = [short(z_ref, 2, b) for b in range(nbatch)]
    for o in range(HY_ORDER):
        hf = jnp.dot(taps_ref[o].astype(BF16), ff_ref[...], preferred_element_type=F32)
        hr, hi = hf[:, :n], hf[:, n:]
        nxt = []
        for b in range(nbatch):
            x = jnp.dot(zz[b].astype(BF16), fh_ref[...], preferred_element_type=F32)
            xr, xi = x[:, :n], x[:, n:]
            y = jnp.concatenate([xr * hr - xi * hi, xr * hi + xi * hr], axis=1)
            conv = jnp.dot(y.astype(BF16), inv_ref[...], preferred_element_type=F32)
            nxt.append(short(gates[o], o, b) * (conv + hb_ref[o] * zz[b]))
        zz = nxt
    for b in range(nbatch):
        o_ref[b] = zz[b].astype(o_ref.dtype)


def _hyena_short(hyt, taps, short_w3, short_b3, hy_bias3, dd, ct):
    b, c3, l = hyt.shape
    ch = c3 // 3
    n = 2 * l
    nct = ch // ct
    blk = (b, ct, l)
    return pl.pallas_call(
        _hyena_short_kernel,
        grid=(nct,),
        in_specs=[pl.BlockSpec((3, 3, ct, 1), lambda j: (0, 0, j, 0)),
                  pl.BlockSpec((3, ct, 1), lambda j: (0, j, 0)),
                  pl.BlockSpec((HY_ORDER, ct, 1), lambda j: (0, j, 0)),
                  pl.BlockSpec(blk, lambda j: (0, j, 0)),
                  pl.BlockSpec(blk, lambda j: (0, nct + j, 0)),
                  pl.BlockSpec(blk, lambda j: (0, 2 * nct + j, 0)),
                  pl.BlockSpec((HY_ORDER, ct, n), lambda j: (0, j, 0)),
                  _full_spec((l, 2 * n)), _full_spec((n, 2 * n)), _full_spec((2 * n, l))],
        out_specs=pl.BlockSpec(blk, lambda j: (0, j, 0)),
        out_shape=jax.ShapeDtypeStruct((b, ch, l), BF16),
        compiler_params=_cparams(1),
        name="hyena_short",
    )(short_w3, short_b3, hy_bias3, hyt, hyt, hyt, taps, dd["fwd_half"], dd["fwd"], dd["inv"])


def _conformer_kernel(up_ref, uc_ref, un_ref, w_ref, b_ref, g_ref, beta_ref, o_ref, buf):
    tm = uc_ref.shape[1]
    ch = o_ref.shape[2]
    j = pl.program_id(1)
    nj = pl.num_programs(1)

    def glu(u):
        u = u.astype(F32)
        return u[:, :ch] * jax.nn.sigmoid(u[:, ch:])

    buf[0:HALO] = jnp.where(j > 0, glu(up_ref[0]), 0.0)
    buf[HALO:HALO + tm] = glu(uc_ref[0])
    buf[HALO + tm:] = jnp.where(j < nj - 1, glu(un_ref[0]), 0.0)
    pad = (CV_K - 1) // 2
    acc = jnp.zeros((tm, ch), F32) + b_ref[...]
    for k in range(CV_K):
        acc = acc + w_ref[k:k + 1, :] * buf[pl.ds(HALO - pad + k, tm), :]
    mu = jnp.mean(acc, axis=-1, keepdims=True)
    cen = acc - mu
    var = jnp.mean(cen * cen, axis=-1, keepdims=True)
    y = cen * lax.rsqrt(var + EPS) * g_ref[...] + beta_ref[...]
    o_ref[0] = _silu(y).astype(o_ref.dtype)


def _conformer(u, dw_w, dw_b, ln_g, ln_b, layer, tm):
    b, l, c2 = u.shape
    ch = c2 // 2
    r = tm // HALO
    nh = l // HALO
    return pl.pallas_call(
        _conformer_kernel,
        grid=(b, l // tm),
        in_specs=[pl.BlockSpec((1, HALO, c2), lambda bi, j: (bi, jnp.maximum(j * r - 1, 0), 0)),
                  pl.BlockSpec((1, tm, c2), lambda bi, j: (bi, j, 0)),
                  pl.BlockSpec((1, HALO, c2), lambda bi, j: (bi, jnp.minimum((j + 1) * r, nh - 1), 0)),
                  _layer_spec((CV_K, ch), layer),
                  _layer_spec((1, ch), layer), _layer_spec((1, ch), layer), _layer_spec((1, ch), layer)],
        out_specs=pl.BlockSpec((1, tm, ch), lambda bi, j: (bi, j, 0)),
        out_shape=jax.ShapeDtypeStruct((b, l, ch), BF16),
        scratch_shapes=[pltpu.VMEM((tm + 2 * HALO, ch), F32)],
        compiler_params=_cparams(2),
        name="conformer_conv",
    )(u, u, u, dw_w, dw_b, ln_g, ln_b)


def _outproj_kernel(a_ref, hy_ref, cv_ref, x_ref, m_ref, g_ref, wa_ref, wh_ref, wc_ref, o_ref):
    y = jnp.dot(a_ref[0], wa_ref[...], preferred_element_type=F32)
    y = y + pl.dot(hy_ref[0], wh_ref[...], trans_a=True)
    y = y + jnp.dot(cv_ref[0], wc_ref[...], preferred_element_type=F32)
    o_ref[0] = x_ref[0] + m_ref[0, 2:3, :] * _rms(y, g_ref[...])


def _outproj(attn, hyt, cv, x, mods, mod_row, g_post, wa, wh, wc, layer, tm):
    b, l, d = x.shape
    mrow = (lambda bi: bi) if mod_row is None else (lambda bi: mod_row)
    return pl.pallas_call(
        _outproj_kernel,
        grid=(b, l // tm),
        in_specs=[pl.BlockSpec((1, tm, attn.shape[2]), lambda bi, j: (bi, j, 0)),
                  pl.BlockSpec((1, hyt.shape[1], tm), lambda bi, j: (bi, 0, j)),
                  pl.BlockSpec((1, tm, cv.shape[2]), lambda bi, j: (bi, j, 0)),
                  pl.BlockSpec((1, tm, d), lambda bi, j: (bi, j, 0)),
                  pl.BlockSpec((1, 6, d), lambda bi, j: (mrow(bi), 0, 0)),
                  _layer_spec((1, d), layer),
                  _layer_spec(wa.shape[1:], layer),
                  _layer_spec(wh.shape[1:], layer),
                  _layer_spec(wc.shape[1:], layer)],
        out_specs=pl.BlockSpec((1, tm, d), lambda bi, j: (bi, j, 0)),
        out_shape=jax.ShapeDtypeStruct((b, l, d), F32),
        compiler_params=_cparams(2),
        name="outproj",
    )(attn, hyt, cv, x, mods, g_post, wa, wh, wc)


def _ffn_kernel(n_chunks, xp_ref, xc_ref, xn_ref, m_ref, gpre_ref, gpost_ref, wu_ref, wg_ref,
                dw_ref, db_ref, wd_ref, o_ref, gbuf):
    tm = xc_ref.shape[1]
    dff = wu_ref.shape[1]
    cw = dff // n_chunks
    j = pl.program_id(1)
    nj = pl.num_programs(1)
    shift, scale, gate = m_ref[0, 3:4, :], m_ref[0, 4:5, :], m_ref[0, 5:6, :]
    gpre = gpre_ref[...]

    def prep(x):
        return (_rms(x, gpre) * (1.0 + scale) + shift).astype(BF16)

    xc = xc_ref[0]
    hb = prep(xc)
    hext = jnp.concatenate([prep(xp_ref[0]), hb, prep(xn_ref[0])], axis=0)
    y = jnp.zeros((tm, o_ref.shape[2]), F32)
    for c in range(n_chunks):
        cols = slice(c * cw, (c + 1) * cw)
        u = jnp.dot(hb, wu_ref[:, cols], preferred_element_type=F32)
        g = jnp.dot(hext, wg_ref[:, cols], preferred_element_type=F32)
        gbuf[0:HALO] = jnp.where(j > 0, g[:HALO], 0.0)
        gbuf[HALO:HALO + tm] = g[HALO:HALO + tm]
        gbuf[HALO + tm:] = jnp.where(j < nj - 1, g[HALO + tm:], 0.0)
        conv = (dw_ref[0:1, cols] * gbuf[pl.ds(HALO - 1, tm), :]
                + dw_ref[1:2, cols] * gbuf[pl.ds(HALO, tm), :]
                + dw_ref[2:3, cols] * gbuf[pl.ds(HALO + 1, tm), :] + db_ref[:, cols])
        act = (_silu(conv) * u).astype(BF16)
        y = y + jnp.dot(act, wd_ref[cols, :], preferred_element_type=F32)
    o_ref[0] = xc + gate * _rms(y, gpost_ref[...])


def _ffn(x, mods, mod_row, g_pre, g_post, wu, wg, dw_w, dw_b, wd, layer, tm, n_chunks):
    b, l, d = x.shape
    dff = wu.shape[2]
    r = tm // HALO
    nh = l // HALO
    mrow = (lambda bi: bi) if mod_row is None else (lambda bi: mod_row)
    return pl.pallas_call(
        functools.partial(_ffn_kernel, n_chunks),
        grid=(b, l // tm),
        in_specs=[pl.BlockSpec((1, HALO, d), lambda bi, j: (bi, jnp.maximum(j * r - 1, 0), 0)),
                  pl.BlockSpec((1, tm, d), lambda bi, j: (bi, j, 0)),
                  pl.BlockSpec((1, HALO, d), lambda bi, j: (bi, jnp.minimum((j + 1) * r, nh - 1), 0)),
                  pl.BlockSpec((1, 6, d), lambda bi, j: (mrow(bi), 0, 0)),
                  _layer_spec((1, d), layer), _layer_spec((1, d), layer),
                  _layer_spec((d, dff), layer), _layer_spec((d, dff), layer),
                  _layer_spec((FFN_K, dff), layer), _layer_spec((1, dff), layer),
                  _layer_spec((dff, d), layer)],
        out_specs=pl.BlockSpec((1, tm, d), lambda bi, j: (bi, j, 0)),
        out_shape=jax.ShapeDtypeStruct((b, l, d), F32),
        scratch_shapes=[pltpu.VMEM((tm + 2 * HALO, dff // n_chunks), F32)],
        compiler_params=_cparams(2),
        name="conv_ffn",
    )(x, x, x, mods, g_pre, g_post, wu, wg, dw_w, dw_b, wd)


def _rope_tables(l):
    rows = l // GRID_W
    row = jnp.repeat(jnp.arange(rows, dtype=F32), GRID_W)
    col = jnp.tile(jnp.arange(GRID_W, dtype=F32), rows)
    inv = ROPE_THETA ** (-jnp.arange(0, AXIS_ROT, 2, dtype=F32) / AXIS_ROT)
    ar, ac = row[:, None] * inv, col[:, None] * inv
    cos = jnp.concatenate([jnp.cos(ar), jnp.cos(ar), jnp.cos(ac), jnp.cos(ac)], axis=1)
    sin = jnp.concatenate([-jnp.sin(ar), jnp.sin(ar), -jnp.sin(ac), jnp.sin(ac)], axis=1)
    rep = LANES // HEAD_DIM
    return jnp.tile(cos, (1, rep)), jnp.tile(sin, (1, rep))


def kernel(x, c, ctx, c_ctx, w_mod, b_mod, g_pre_mix, g_post_mix, g_pre_ffn, g_post_ffn, w_in, attn_sink, hy_short_w, hy_short_b, hy_w1, hy_b1, hy_freq1, hy_w2, hy_b2, hy_freq2, hy_w3, hy_bias, cv_dw_w, cv_dw_b, cv_ln_g, cv_ln_b, w_out, w_up, ffn_dw_w, ffn_dw_b, w_down):
    b, l, d = x.shape
    lc = ctx.shape[1]
    depth = w_in.shape[0]
    hy_ch = hy_bias.shape[2]
    cv_ch = cv_dw_w.shape[2]
    dff = w_down.shape[1]
    k0 = ATTN_W
    hy0 = k0 + 2 * KV_W
    cv0 = hy0 + (HY_ORDER + 1) * hy_ch
    fw = hy_w2.shape[1]

    wqkv = w_in[:, :, :hy0].astype(BF16)
    whyt = jnp.swapaxes(w_in[:, :, hy0:cv0], 1, 2).astype(BF16)
    wcv = w_in[:, :, cv0:].astype(BF16)
    wo_a = w_out[:, :ATTN_W].astype(BF16)
    wo_h = w_out[:, ATTN_W:ATTN_W + hy_ch].astype(BF16)
    wo_c = w_out[:, ATTN_W + hy_ch:].astype(BF16)
    wu = w_up[:, :, :dff].astype(BF16)
    wg = w_up[:, :, dff:].astype(BF16)
    wd = w_down.astype(BF16)
    row = lambda a: a[:, None, :]
    gpm, gqm, gpf, gqf = row(g_pre_mix), row(g_post_mix), row(g_pre_ffn), row(g_post_ffn)
    w1p = jnp.pad(hy_w1, ((0, 0), (0, LANES - HY_EMB), (0, 0)))
    w3t = jnp.swapaxes(hy_w3, 1, 2)
    deltas = jnp.abs(jnp.linspace(math.log(HY_TARGET) / HY_SLOW, math.log(HY_TARGET) / HY_FAST,
                                  hy_ch, dtype=F32))[:, None]
    short_b2 = row(hy_short_b)
    sw5 = jnp.transpose(hy_short_w.reshape(depth, 3, 3, hy_ch), (0, 2, 1, 3))[..., None]
    sb4 = hy_short_b.reshape(depth, 3, hy_ch)[..., None]
    hb4 = hy_bias[..., None]

    n_rows = 8 * ((b + 1 + 7) // 8)
    cond = jnp.zeros((n_rows, d), F32).at[:b].set(c).at[b].set(c_ctx)
    mods = _modulation(cond, w_mod, b_mod).reshape(depth, n_rows, 6, d)

    cs, sn = _rope_tables(l)
    tab_l, tab_c = _filter_tables(l), _filter_tables(lc)
    dc = _dft_consts(2 * l // LANES, LANES)
    dd = _dense_dft_consts(lc)
    tm = 512
    tmc = lc

    for i in range(depth):
        last = i == depth - 1
        m_i = mods[i]
        hyp = (row(hy_b1), row(hy_freq1), hy_w2, row(hy_b2), row(hy_freq2), w3t, deltas)

        q, k, v, hyt, cvu = _inproj(x, m_i, None, gpm, wqkv, whyt, wcv, i, cs, sn, tm)
        qc, kc, vc, hyt_c, cvu_c = _inproj(ctx, m_i, b, gpm, wqkv, whyt, wcv, i, None, None, tmc)

        attn = _window_attention(q, k, v, kc, vc, attn_sink[i][None, :], 256)
        taps = _hyena_taps(l, tab_l, w1p, *hyp, i)
        hf = _filter_spectrum(taps, dc)
        hy = _hyena_long(hyt, hf, hy_short_w[i], short_b2[i], hy_bias[i], dc)
        cv = _conformer(cvu, cv_dw_w, row(cv_dw_b), row(cv_ln_g), row(cv_ln_b), i, tm)
        x = _outproj(attn, hy, cv, x, m_i, None, gqm, wo_a, wo_h, wo_c, i, tm)
        x = _ffn(x, m_i, None, gpf, gqf, wu, wg, ffn_dw_w, row(ffn_dw_b), wd, i, tm, 2)

        if not last:
            attn_c = _context_attention(qc, kc, vc, attn_sink[i][None, :])
            taps_c = _hyena_taps(lc, tab_c, w1p, *hyp, i)
            hy_c = _hyena_short(hyt_c, taps_c, sw5[i], sb4[i], hb4[i], dd, 32)
            cv_c = _conformer(cvu_c, cv_dw_w, row(cv_dw_b), row(cv_ln_g), row(cv_ln_b), i, tmc)
            ctx = _outproj(attn_c, hy_c, cv_c, ctx, m_i, b, gqm, wo_a, wo_h, wo_c, i, tmc)
            ctx = _ffn(ctx, m_i, b, gpf, gqf, wu, wg, ffn_dw_w, row(ffn_dw_b), wd, i, tmc, 2)
    return x
```

```python
import functools
import math

import numpy as np
import jax
import jax.numpy as jnp
from jax import lax
from jax.experimental import pallas as pl
from jax.experimental.pallas import tpu as pltpu

F32 = jnp.float32
BF16 = jnp.bfloat16
HIGHEST = lax.Precision.HIGHEST

N_HEADS = 8
N_KV_HEADS = 2
HEAD_DIM = 64
GROUP = N_HEADS // N_KV_HEADS
ATTN_W = N_HEADS * HEAD_DIM
KV_W = N_KV_HEADS * HEAD_DIM
WINDOW = 128
GRID_W = 64
ROPE_THETA = 10000.0
AXIS_ROT = HEAD_DIM // 2
HY_ORDER = 2
HY_EMB = 33
HY_BANDS = (HY_EMB - 1) // 2
HY_TARGET = 1e-2
HY_FAST = 0.3
HY_SLOW = 1.5
CV_K = 31
FFN_K = 3
EPS = 1e-6
NEG_INF = -1e30
LOG2E = math.log2(math.e)

LANES = 128
SUBLANES = 8
HALO = 16
VMEM_LIMIT = 56 * 1024 * 1024

NT = (((1,), (1,)), ((), ()))


def _cparams(n_axes):
    return pltpu.CompilerParams(dimension_semantics=("parallel",) * n_axes,
                                vmem_limit_bytes=VMEM_LIMIT)


def _full_spec(shape):
    nd = len(shape)
    return pl.BlockSpec(shape, lambda *_: (0,) * nd)


def _layer_spec(shape, layer):
    nd = len(shape)
    return pl.BlockSpec((None,) + tuple(shape), lambda *_: (layer,) + (0,) * nd)


def _rms(t, g):
    return t * lax.rsqrt(jnp.mean(t * t, axis=-1, keepdims=True) + EPS) * g


def _silu(t):
    return t * jax.nn.sigmoid(t)


def _mod_kernel(c_ref, w_ref, b_ref, o_ref):
    s = _silu(c_ref[...])
    o_ref[0] = jnp.dot(s, w_ref[0], preferred_element_type=F32, precision=HIGHEST) + b_ref[0]


def _modulation(cond, w_mod, b_mod):
    depth, d, n = w_mod.shape
    rows = cond.shape[0]
    tn = 1536
    assert n % tn == 0
    return pl.pallas_call(
        _mod_kernel,
        grid=(depth, n // tn),
        in_specs=[pl.BlockSpec((rows, d), lambda i, j: (0, 0)),
                  pl.BlockSpec((1, d, tn), lambda i, j: (i, 0, j)),
                  pl.BlockSpec((1, 1, tn), lambda i, j: (i, 0, j))],
        out_specs=pl.BlockSpec((1, rows, tn), lambda i, j: (i, 0, j)),
        out_shape=jax.ShapeDtypeStruct((depth, rows, n), F32),
        compiler_params=_cparams(2),
        name="modulation",
    )(cond, w_mod, b_mod.reshape(depth, 1, n))


def _swap_halves(t):
    w = t.shape[-1]
    lane = lax.broadcasted_iota(jnp.int32, t.shape, 1)
    from_hi = pltpu.roll(t, w - AXIS_ROT // 2, axis=1)
    from_lo = pltpu.roll(t, AXIS_ROT // 2, axis=1)
    return jnp.where(lane % AXIS_ROT < AXIS_ROT // 2, from_hi, from_lo)


def _inproj_kernel(rope, x_ref, m_ref, g_ref, wqk_ref, wt_ref, wcv_ref, cs_ref, sn_ref,
                   q_ref, k_ref, vt_ref, hy_ref, cv_ref):
    x = x_ref[0]
    h = _rms(x, g_ref[...]) * (1.0 + m_ref[0, 1:2, :]) + m_ref[0, 0:1, :]
    hb = h.astype(BF16)
    p = jnp.dot(hb, wqk_ref[...], preferred_element_type=F32)
    pt = lax.dot_general(wt_ref[...], hb, NT, preferred_element_type=F32)
    q = p[:, :ATTN_W]
    k = p[:, ATTN_W:]
    if rope:
        cs = cs_ref[...]
        sn = sn_ref[...]
        cq = jnp.concatenate([cs] * (ATTN_W // LANES), axis=1)
        sq = jnp.concatenate([sn] * (ATTN_W // LANES), axis=1)
        q = q * cq + _swap_halves(q) * sq
        k = k * cs + _swap_halves(k) * sn
    q_ref[0] = (q * (HEAD_DIM ** -0.5 * LOG2E)).astype(BF16)
    k_ref[0] = jnp.concatenate([k, pltpu.roll(k, HEAD_DIM, axis=1)], axis=1).astype(BF16)
    vt_ref[0] = jnp.concatenate([pt[:KV_W], pt[HEAD_DIM:KV_W], pt[:HEAD_DIM]], axis=0).astype(BF16)
    hy_ref[0] = pt[KV_W:].astype(BF16)
    cv_ref[0] = jnp.dot(hb, wcv_ref[...], preferred_element_type=F32).astype(BF16)


def _inproj(x, mods, mod_row, g_pre, wqkv, whyt, wcv, layer, cs, sn, tm):
    b, l, d = x.shape
    rope = cs is not None
    if not rope:
        cs = jnp.zeros((l, LANES), F32)
        sn = cs
    nhy = whyt.shape[1] - KV_W
    ncv = wcv.shape[2]
    mrow = (lambda bi: bi) if mod_row is None else (lambda bi: mod_row)
    return pl.pallas_call(
        functools.partial(_inproj_kernel, rope),
        grid=(b, l // tm),
        in_specs=[pl.BlockSpec((1, tm, d), lambda bi, j: (bi, j, 0)),
                  pl.BlockSpec((1, 6, d), lambda bi, j: (mrow(bi), 0, 0)),
                  _layer_spec((1, d), layer),
                  _layer_spec(wqkv.shape[1:], layer),
                  _layer_spec(whyt.shape[1:], layer),
                  _layer_spec(wcv.shape[1:], layer),
                  pl.BlockSpec((tm, LANES), lambda bi, j: (j, 0)),
                  pl.BlockSpec((tm, LANES), lambda bi, j: (j, 0))],
        out_specs=[pl.BlockSpec((1, tm, ATTN_W), lambda bi, j: (bi, j, 0)),
                   pl.BlockSpec((1, tm, 2 * KV_W), lambda bi, j: (bi, j, 0)),
                   pl.BlockSpec((1, 2 * KV_W, tm), lambda bi, j: (bi, 0, j)),
                   pl.BlockSpec((1, nhy, tm), lambda bi, j: (bi, 0, j)),
                   pl.BlockSpec((1, tm, ncv), lambda bi, j: (bi, j, 0))],
        out_shape=[jax.ShapeDtypeStruct((b, l, ATTN_W), BF16),
                   jax.ShapeDtypeStruct((b, l, 2 * KV_W), BF16),
                   jax.ShapeDtypeStruct((b, 2 * KV_W, l), BF16),
                   jax.ShapeDtypeStruct((b, nhy, l), BF16),
                   jax.ShapeDtypeStruct((b, l, ncv), BF16)],
        compiler_params=_cparams(2),
        name="inproj",
    )(x, mods, g_pre, wqkv, whyt, wcv, cs, sn)


def _half_variants(a, axis):
    idx = lax.broadcasted_iota(jnp.int32, (LANES, 1) if axis == 0 else (1, LANES), axis)
    lo = idx < HEAD_DIM
    x, y = (a[:LANES], a[LANES:]) if axis == 0 else (a[:, :LANES], a[:, LANES:])
    zero = jnp.zeros_like(x)
    return {(0, 0): jnp.where(lo, x, zero), (0, 1): jnp.where(lo, zero, y),
            (1, 0): jnp.where(lo, y, zero), (1, 1): jnp.where(lo, zero, x)}


def _attend(jobs, q, sink_ref, o_ref):
    row_lo = lax.broadcasted_iota(jnp.int32, (LANES, 1), 0) < HEAD_DIM
    units = []
    for row0, rows, kvars, vtvars, masks in jobs:
        for kv in range(N_KV_HEADS):
            slabs = (2 * kv, 2 * kv + 1)
            qs = jnp.concatenate([q[row0:row0 + rows, s * LANES:(s + 1) * LANES] for s in slabs], axis=0)
            kcat = jnp.concatenate([kd[(kv, half)] for half in range(2) for kd in kvars], axis=0)
            s_all = lax.dot_general(kcat, qs, NT, preferred_element_type=F32)
            units.append((row0, rows, kv, slabs, kvars, vtvars, masks, s_all))
    for row0, rows, kv, slabs, kvars, vtvars, masks, s_all in units:
        vtcat = jnp.concatenate([vd[(kv, half)] for half in range(2) for vd in vtvars], axis=1)
        probs, denoms = [], []
        off = 0
        for half in range(2):
            sink = jnp.concatenate([jnp.full((1, rows), sink_ref[0, 2 * s + half] * LOG2E, F32)
                                    for s in slabs], axis=1)
            m = sink
            segs = []
            for kd, mk in zip(kvars, masks):
                n = kd[(kv, half)].shape[0]
                s = s_all[off:off + n]
                off += n
                if mk is not None:
                    s = jnp.where(mk, s, NEG_INF)
                segs.append(s)
                m = jnp.maximum(m, jnp.max(s, axis=0, keepdims=True))
            denom = jnp.exp2(sink - m)
            for s in segs:
                p = jnp.exp2(s - m)
                denom = denom + jnp.sum(p, axis=0, keepdims=True)
                probs.append(p.astype(BF16))
            denoms.append(denom)
        ot = jnp.dot(vtcat, jnp.concatenate(probs, axis=0), preferred_element_type=F32)
        ot = ot / jnp.where(row_lo, denoms[0], denoms[1])
        for t, s in enumerate(slabs):
            o_ref[0, row0:row0 + rows, s * LANES:(s + 1) * LANES] = (
                ot[:, t * rows:(t + 1) * rows].T.astype(o_ref.dtype))


def _win_attn_kernel(sink_ref, q_ref, kp_ref, kc_ref, kn_ref, vp_ref, vc_ref, vn_ref,
                     kx_ref, vx_ref, o_ref):
    tq = q_ref.shape[1]
    j = pl.program_id(1)
    nj = pl.num_programs(1)
    n_sub = tq // WINDOW
    span = 3 * WINDOW
    q = q_ref[0]
    kloc = _half_variants(jnp.concatenate([kp_ref[0], kc_ref[0], kn_ref[0]], axis=0), 1)
    vloc = _half_variants(jnp.concatenate([vp_ref[0], vc_ref[0], vn_ref[0]], axis=1), 0)
    kctx = _half_variants(kx_ref[0], 1)
    vctx = _half_variants(vx_ref[0], 0)
    c = lax.broadcasted_iota(jnp.int32, (span, 2 * WINDOW), 0)
    r = lax.broadcasted_iota(jnp.int32, (span, 2 * WINDOW), 1) % WINDOW
    band = (c >= r) & (c <= r + 2 * WINDOW)
    jobs = []
    for sb in range(n_sub):
        lo = jnp.where((j == 0) & (sb == 0), WINDOW, 0)
        hi = jnp.where((j == nj - 1) & (sb == n_sub - 1), 2 * WINDOW, span)
        mask = band & (c >= lo) & (c < hi)
        ks = {key: val[sb * WINDOW:sb * WINDOW + span] for key, val in kloc.items()}
        vs = {key: val[:, sb * WINDOW:sb * WINDOW + span] for key, val in vloc.items()}
        jobs.append((sb * WINDOW, WINDOW, [ks, kctx], [vs, vctx], [mask, None]))
    _attend(jobs, q, sink_ref, o_ref)


def _window_attention(q, k, v, k_ctx, v_ctx, sink, tq):
    b, l, _ = q.shape
    n_ctx = k_ctx.shape[1]
    assert tq % WINDOW == 0 and l % tq == 0
    r = tq // WINDOW
    nwb = l // WINDOW
    kvw = k.shape[2]
    i_prev = lambda j: jnp.maximum(j * r - 1, 0)
    i_next = lambda j: jnp.minimum((j + 1) * r, nwb - 1)
    prev = pl.BlockSpec((1, WINDOW, kvw), lambda bi, j: (bi, i_prev(j), 0))
    cur = pl.BlockSpec((1, tq, kvw), lambda bi, j: (bi, j, 0))
    nxt = pl.BlockSpec((1, WINDOW, kvw), lambda bi, j: (bi, i_next(j), 0))
    ctx = pl.BlockSpec((1, n_ctx, kvw), lambda bi, j: (bi, 0, 0))
    prev_t = pl.BlockSpec((1, kvw, WINDOW), lambda bi, j: (bi, 0, i_prev(j)))
    cur_t = pl.BlockSpec((1, kvw, tq), lambda bi, j: (bi, 0, j))
    nxt_t = pl.BlockSpec((1, kvw, WINDOW), lambda bi, j: (bi, 0, i_next(j)))
    ctx_t = pl.BlockSpec((1, kvw, n_ctx), lambda bi, j: (bi, 0, 0))
    return pl.pallas_call(
        _win_attn_kernel,
        grid=(b, l // tq),
        in_specs=[pl.BlockSpec(memory_space=pltpu.SMEM),
                  pl.BlockSpec((1, tq, ATTN_W), lambda bi, j: (bi, j, 0)),
                  prev, cur, nxt, prev_t, cur_t, nxt_t, ctx, ctx_t],
        out_specs=pl.BlockSpec((1, tq, ATTN_W), lambda bi, j: (bi, j, 0)),
        out_shape=jax.ShapeDtypeStruct((b, l, ATTN_W), BF16),
        compiler_params=_cparams(2),
        name="window_attention",
    )(sink, q, k, k, k, v, v, v, k_ctx, v_ctx)


def _ctx_attn_kernel(sink_ref, q_ref, k_ref, v_ref, o_ref):
    job = (0, q_ref.shape[1], [_half_variants(k_ref[0], 1)], [_half_variants(v_ref[0], 0)], [None])
    _attend([job], q_ref[0], sink_ref, o_ref)


def _context_attention(q, k, v, sink):
    b, l, _ = q.shape
    kvw = k.shape[2]
    return pl.pallas_call(
        _ctx_attn_kernel,
        grid=(b,),
        in_specs=[pl.BlockSpec(memory_space=pltpu.SMEM),
                  pl.BlockSpec((1, l, ATTN_W), lambda bi: (bi, 0, 0)),
                  pl.BlockSpec((1, l, kvw), lambda bi: (bi, 0, 0)),
                  pl.BlockSpec((1, kvw, l), lambda bi: (bi, 0, 0))],
        out_specs=pl.BlockSpec((1, l, ATTN_W), lambda bi: (bi, 0, 0)),
        out_shape=jax.ShapeDtypeStruct((b, l, ATTN_W), BF16),
        compiler_params=_cparams(1),
        name="context_attention",
    )(sink, q, k, v)


def _taps_kernel(seq_len, z_ref, w1_ref, b1_ref, f1_ref, w2_ref, b2_ref, f2_ref, w3t_ref,
                 dl_ref, o_ref):
    tn = z_ref.shape[1]
    ch = dl_ref.shape[0]
    z = z_ref[...]
    h = jnp.sin(f1_ref[...] * (jnp.dot(w1_ref[...], z, preferred_element_type=F32,
                                       precision=HIGHEST) + b1_ref[...]))
    h = jnp.sin(f2_ref[...] * (jnp.dot(w2_ref[...], h, preferred_element_type=F32,
                                       precision=HIGHEST) + b2_ref[...]))
    full = jnp.dot(w3t_ref[...], h.astype(BF16), preferred_element_type=F32)
    n0 = pl.program_id(0) * tn
    is_fwd = n0 < seq_len
    decay = jnp.exp(-z[0:1, :] * dl_ref[...])
    pos = n0 + lax.broadcasted_iota(jnp.int32, (1, tn), 1)
    for o in range(HY_ORDER):
        fwd = full[o * 2 * ch:o * 2 * ch + ch]
        bwd = full[o * 2 * ch + ch:(o + 1) * 2 * ch]
        taps = jnp.where(is_fwd, fwd, bwd) * decay
        o_ref[o] = jnp.where(pos == seq_len, 0.0, taps).astype(o_ref.dtype)


def _filter_tables(l):
    t = jnp.linspace(0.0, 1.0, l, dtype=F32)[:, None]
    w = (2.0 * math.pi / l) * jnp.arange(l, dtype=F32)[:, None]
    f = jnp.linspace(1e-4, HY_BANDS - 1, HY_BANDS, dtype=F32)[None, :]
    z = jnp.concatenate([t, jnp.cos(f * w), -jnp.sin(f * w)], axis=-1)
    n = np.arange(2 * l)
    idx = np.minimum(np.where(n < l, n, 2 * l - n), l - 1)
    return jnp.pad(z[idx], ((0, 0), (0, LANES - HY_EMB))).T


def _hyena_taps(l, zct, w1t, b1, f1, w2t, b2, f2, w3t, deltas, layer):
    ch = deltas.shape[0]
    fw = w2t.shape[1]
    tn = min(1024, l)
    return pl.pallas_call(
        functools.partial(_taps_kernel, l),
        grid=(2 * l // tn,),
        in_specs=[pl.BlockSpec((LANES, tn), lambda j: (0, j)),
                  _layer_spec((fw, LANES), layer),
                  _layer_spec((fw, 1), layer),
                  _layer_spec((fw, 1), layer),
                  _layer_spec((fw, fw), layer),
                  _layer_spec((fw, 1), layer),
                  _layer_spec((fw, 1), layer),
                  _layer_spec(w3t.shape[1:], layer),
                  _full_spec((ch, 1))],
        out_specs=pl.BlockSpec((HY_ORDER, ch, tn), lambda j: (0, 0, j)),
        out_shape=jax.ShapeDtypeStruct((HY_ORDER, ch, 2 * l), BF16),
        compiler_params=_cparams(1),
        name="hyena_taps",
    )(zct, w1t, b1, f1, w2t, b2, f2, w3t, deltas)


def _dft_consts(na, nb):
    a = np.arange(na)
    fa = np.exp(-2j * np.pi * np.outer(a, a) / na)
    b = np.arange(nb)
    g = np.exp(-2j * np.pi * np.outer(b, b) / nb)
    fr, fi = fa.real, fa.imag
    gr, gi = g.real, g.imag
    h = na // 2
    m1 = np.block([[fr[:, :h], -fi[:, :h]], [fi[:, :h], fr[:, :h]]])
    m1f = np.concatenate([fr, fi], axis=0)
    m2 = np.block([[gr, gi], [-gi, gr]])
    m2i = np.block([[gr, -gi], [gi, gr]])
    m1i = np.block([[fr[:h], fi[:h]], [-fi[:h], fr[:h]]])
    tw = np.exp(-2j * np.pi * np.outer(a, b) / (na * nb))
    bf = lambda m: jnp.asarray(m, dtype=F32).astype(BF16)
    return dict(m1=bf(m1), m1f=bf(m1f), m2=bf(m2), m2i=bf(m2i), m1i=bf(m1i),
                tr=jnp.asarray(tw.real, F32), ti=jnp.asarray(tw.imag, F32))


def _stage2(a_all, tr, ti, m2):
    na, nb = tr.shape
    n = a_all.shape[1] // nb
    lhs = []
    for i in range(n):
        ar = a_all[:na, i * nb:(i + 1) * nb]
        ai = a_all[na:, i * nb:(i + 1) * nb]
        lhs.append(jnp.concatenate([ar * tr - ai * ti, ar * ti + ai * tr], axis=1).astype(BF16))
    return jnp.dot(jnp.concatenate(lhs, axis=0), m2, preferred_element_type=F32)


def _filter_spectra(taps, m1f, m2, tr, ti):
    na, nb = tr.shape
    a_all = jnp.dot(m1f, jnp.concatenate(taps, axis=1), preferred_element_type=F32)
    x = _stage2(a_all, tr, ti, m2) * (1.0 / (na * nb))
    return [x[i * na:(i + 1) * na] for i in range(len(taps))]


def _shift_prev(x):
    rows, nb = x.shape
    lane = lax.broadcasted_iota(jnp.int32, x.shape, 1)
    row = lax.broadcasted_iota(jnp.int32, x.shape, 0)
    r = pltpu.roll(x, 1, axis=1)
    r2 = pltpu.roll(r, 1, axis=0)
    return jnp.where(lane == 0, jnp.where(row == 0, 0.0, r2), r)


def _shift_next(x):
    rows, nb = x.shape
    lane = lax.broadcasted_iota(jnp.int32, x.shape, 1)
    row = lax.broadcasted_iota(jnp.int32, x.shape, 0)
    r = pltpu.roll(x, nb - 1, axis=1)
    r2 = pltpu.roll(r, rows - 1, axis=0)
    return jnp.where(lane == nb - 1, jnp.where(row == rows - 1, 0.0, r2), r)


def _fft_conv(zs, hfs, m1, m2, m2i, m1i, tr, ti):
    na, nb = tr.shape
    n = len(zs)
    h = na // 2
    zc = jnp.concatenate([jnp.concatenate([zr, zi], axis=0) for zr, zi in zs], axis=1)
    a_all = jnp.dot(m1, zc.astype(BF16), preferred_element_type=F32)
    x_all = _stage2(a_all, tr, ti, m2)
    ys = []
    for i in range(n):
        x = x_all[i * na:(i + 1) * na]
        xr, xi = x[:, :nb], x[:, nb:]
        hr, hi = hfs[i][:, :nb], hfs[i][:, nb:]
        ys.append(jnp.concatenate([xr * hr - xi * hi, xr * hi + xi * hr], axis=1).astype(BF16))
    b_all = jnp.dot(jnp.concatenate(ys, axis=0), m2i, preferred_element_type=F32)
    rhs = []
    for i in range(n):
        bm = b_all[i * na:(i + 1) * na]
        br, bi = bm[:, :nb], bm[:, nb:]
        rhs.append(jnp.concatenate([br * tr + bi * ti, bi * tr - br * ti], axis=0).astype(BF16))
    yo = jnp.dot(m1i, jnp.concatenate(rhs, axis=1), preferred_element_type=F32)
    return [(yo[:h, i * nb:(i + 1) * nb], yo[h:, i * nb:(i + 1) * nb]) for i in range(n)]


def _hyena_kernel(sw_ref, sb_ref, hb_ref, x1_ref, x2_ref, z_ref, taps_ref,
                  m1_ref, m1f_ref, m2_ref, m2i_ref, m1i_ref, tr_ref, ti_ref, o_ref):
    nbatch, ct = x1_ref.shape[:2]
    nch = hb_ref.shape[1]
    tr, ti = tr_ref[...], ti_ref[...]
    mats = (m1_ref[...], m2_ref[...], m2i_ref[...], m1i_ref[...], tr, ti)
    gates = (x1_ref, x2_ref)
    c0 = pl.program_id(0) * ct

    def short(ref, stream, b, c):
        ch = stream * nch + c0 + c
        x = ref[b, c].astype(F32)
        return (sw_ref[0, ch] * _shift_prev(x) + sw_ref[1, ch] * x
                + sw_ref[2, ch] * _shift_next(x) + sb_ref[0, ch])

    specs = _filter_spectra([taps_ref[o, c] for o in range(HY_ORDER) for c in range(ct)],
                            m1f_ref[...], m2_ref[...], tr, ti)
    seqs = [(c, p) for c in range(ct) for p in range(nbatch // 2)]
    zz = [(short(z_ref, 2, 2 * p, c), short(z_ref, 2, 2 * p + 1, c)) for c, p in seqs]
    for o in range(HY_ORDER):
        ys = _fft_conv(zz, [specs[o * ct + c] for c, _ in seqs], *mats)
        zz = [tuple(short(gates[o], o, 2 * p + q, c) * (ys[i][q] + hb_ref[o, c0 + c] * zz[i][q])
                    for q in range(2)) for i, (c, p) in enumerate(seqs)]
    for i, (c, p) in enumerate(seqs):
        for q in range(2):
            o_ref[2 * p + q, c] = zz[i][q].astype(o_ref.dtype)


def _hyena_long(hyt, taps, short_w, short_b, hy_bias, dc, ct):
    b, c3, l = hyt.shape
    ch = c3 // 3
    nb = LANES
    na = 2 * l // nb
    assert b % 2 == 0 and ct % 2 == 0 and ch % ct == 0
    h4 = hyt.reshape(b, c3, na // 2, nb)
    t4 = taps.reshape(HY_ORDER, ch, na, nb)
    blk = (b, ct, na // 2, nb)
    nct = ch // ct
    smem = pl.BlockSpec(memory_space=pltpu.SMEM)
    out = pl.pallas_call(
        _hyena_kernel,
        grid=(nct,),
        in_specs=[smem, smem, smem,
                  pl.BlockSpec(blk, lambda j: (0, j, 0, 0)),
                  pl.BlockSpec(blk, lambda j: (0, nct + j, 0, 0)),
                  pl.BlockSpec(blk, lambda j: (0, 2 * nct + j, 0, 0)),
                  pl.BlockSpec((HY_ORDER, ct, na, nb), lambda j: (0, j, 0, 0)),
                  _full_spec((2 * na, na)), _full_spec((2 * na, na)),
                  _full_spec((2 * nb, 2 * nb)), _full_spec((2 * nb, 2 * nb)),
                  _full_spec((na, 2 * na)), _full_spec((na, nb)), _full_spec((na, nb))],
        out_specs=pl.BlockSpec(blk, lambda j: (0, j, 0, 0)),
        out_shape=jax.ShapeDtypeStruct((b, ch, na // 2, nb), BF16),
        compiler_params=_cparams(1),
        name="hyena_long",
    )(short_w, short_b, hy_bias, h4, h4, h4, t4,
      dc["m1"], dc["m1f"], dc["m2"], dc["m2i"], dc["m1i"], dc["tr"], dc["ti"])
    return out.reshape(b, ch, l)


def _dense_dft_consts(l):
    n = 2 * l
    k = np.arange(n)
    ang = 2 * np.pi * np.outer(k, k) / n
    cos, sin = np.cos(ang), np.sin(ang)
    fwd = np.concatenate([cos, -sin], axis=1)
    inv = np.concatenate([cos[:, :l], -sin[:, :l]], axis=0) / n
    bf = lambda m: jnp.asarray(m, dtype=F32).astype(BF16)
    return dict(fwd_half=bf(fwd[:l]), fwd=bf(fwd), inv=bf(inv))


def _hyena_short_kernel(sw_ref, sb_ref, hb_ref, x1_ref, x2_ref, z_ref, taps_ref,
                        fh_ref, ff_ref, inv_ref, o_ref):
    nbatch, ct, l = x1_ref.shape
    n = 2 * l
    lane = lax.broadcasted_iota(jnp.int32, (ct, l), 1)

    def short(ref, stream, b):
        x = ref[b].astype(F32)
        prev = jnp.where(lane == 0, 0.0, pltpu.roll(x, 1, axis=1))
        nxt = jnp.where(lane == l - 1, 0.0, pltpu.roll(x, l - 1, axis=1))
        return sw_ref[stream, 0] * prev + sw_ref[stream, 1] * x + sw_ref[stream, 2] * nxt + sb_ref[stream]

    gates = (x1_ref, x2_ref)
    zz = [short(z_ref, 2, b) for b in range(nbatch)]
    for o in range(HY_ORDER):
        hf = jnp.dot(taps_ref[o].astype(BF16), ff_ref[...], preferred_element_type=F32)
        hr, hi = hf[:, :n], hf[:, n:]
        nxt = []
        for b in range(nbatch):
            x = jnp.dot(zz[b].astype(BF16), fh_ref[...], preferred_element_type=F32)
            xr, xi = x[:, :n], x[:, n:]
            y = jnp.concatenate([xr * hr - xi * hi, xr * hi + xi * hr], axis=1)
            conv = jnp.dot(y.astype(BF16), inv_ref[...], preferred_element_type=F32)
            nxt.append(short(gates[o], o, b) * (conv + hb_ref[o] * zz[b]))
        zz = nxt
    for b in range(nbatch):
        o_ref[b] = zz[b].astype(o_ref.dtype)


def _hyena_short(hyt, taps, short_w3, short_b3, hy_bias3, dd, ct):
    b, c3, l = hyt.shape
    ch = c3 // 3
    n = 2 * l
    nct = ch // ct
    blk = (b, ct, l)
    return pl.pallas_call(
        _hyena_short_kernel,
        grid=(nct,),
        in_specs=[pl.BlockSpec((3, 3, ct, 1), lambda j: (0, 0, j, 0)),
                  pl.BlockSpec((3, ct, 1), lambda j: (0, j, 0)),
                  pl.BlockSpec((HY_ORDER, ct, 1), lambda j: (0, j, 0)),
                  pl.BlockSpec(blk, lambda j: (0, j, 0)),
                  pl.BlockSpec(blk, lambda j: (0, nct + j, 0)),
                  pl.BlockSpec(blk, lambda j: (0, 2 * nct + j, 0)),
                  pl.BlockSpec((HY_ORDER, ct, n), lambda j: (0, j, 0)),
                  _full_spec((l, 2 * n)), _full_spec((n, 2 * n)), _full_spec((2 * n, l))],
        out_specs=pl.BlockSpec(blk, lambda j: (0, j, 0)),
        out_shape=jax.ShapeDtypeStruct((b, ch, l), BF16),
        compiler_params=_cparams(1),
        name="hyena_short",
    )(short_w3, short_b3, hy_bias3, hyt, hyt, hyt, taps, dd["fwd_half"], dd["fwd"], dd["inv"])


def _conformer_kernel(up_ref, uc_ref, un_ref, w_ref, b_ref, g_ref, beta_ref, o_ref, buf):
    tm = uc_ref.shape[1]
    ch = o_ref.shape[2]
    j = pl.program_id(1)
    nj = pl.num_programs(1)

    def glu(u):
        u = u.astype(F32)
        return u[:, :ch] * jax.nn.sigmoid(u[:, ch:])

    buf[0, 0:HALO] = jnp.where(j > 0, glu(up_ref[0]), 0.0)
    buf[0, HALO:HALO + tm] = glu(uc_ref[0])
    buf[0, HALO + tm:] = jnp.where(j < nj - 1, glu(un_ref[0]), 0.0)
    span = tm + 2 * HALO - SUBLANES
    for s in range(1, SUBLANES):
        buf[s, 0:span] = buf[0, pl.ds(s, span), :]
    pad = (CV_K - 1) // 2
    rc = 128
    for r0 in range(0, tm, rc):
        acc = jnp.zeros((rc, ch), F32) + b_ref[...]
        for k in range(CV_K):
            off = HALO - pad + k
            acc = acc + w_ref[k:k + 1, :] * buf[off % SUBLANES, pl.ds(r0 + off - off % SUBLANES, rc), :]
        mu = jnp.mean(acc, axis=-1, keepdims=True)
        cen = acc - mu
        var = jnp.mean(cen * cen, axis=-1, keepdims=True)
        y = cen * lax.rsqrt(var + EPS) * g_ref[...] + beta_ref[...]
        o_ref[0, r0:r0 + rc] = _silu(y).astype(o_ref.dtype)


def _conformer(u, dw_w, dw_b, ln_g, ln_b, layer, tm):
    b, l, c2 = u.shape
    ch = c2 // 2
    r = tm // HALO
    nh = l // HALO
    return pl.pallas_call(
        _conformer_kernel,
        grid=(b, l // tm),
        in_specs=[pl.BlockSpec((1, HALO, c2), lambda bi, j: (bi, jnp.maximum(j * r - 1, 0), 0)),
                  pl.BlockSpec((1, tm, c2), lambda bi, j: (bi, j, 0)),
                  pl.BlockSpec((1, HALO, c2), lambda bi, j: (bi, jnp.minimum((j + 1) * r, nh - 1), 0)),
                  _layer_spec((CV_K, ch), layer),
                  _layer_spec((1, ch), layer), _layer_spec((1, ch), layer), _layer_spec((1, ch), layer)],
        out_specs=pl.BlockSpec((1, tm, ch), lambda bi, j: (bi, j, 0)),
        out_shape=jax.ShapeDtypeStruct((b, l, ch), BF16),
        scratch_shapes=[pltpu.VMEM((SUBLANES, tm + 2 * HALO, ch), F32)],
        compiler_params=_cparams(2),
        name="conformer_conv",
    )(u, u, u, dw_w, dw_b, ln_g, ln_b)


def _outproj_kernel(a_ref, hy_ref, cv_ref, x_ref, m_ref, g_ref, wa_ref, wh_ref, wc_ref, o_ref):
    y = jnp.dot(a_ref[0], wa_ref[...], preferred_element_type=F32)
    y = y + pl.dot(hy_ref[0], wh_ref[...], trans_a=True)
    y = y + jnp.dot(cv_ref[0], wc_ref[...], preferred_element_type=F32)
    o_ref[0] = x_ref[0] + m_ref[0, 2:3, :] * _rms(y, g_ref[...])


def _outproj(attn, hyt, cv, x, mods, mod_row, g_post, wa, wh, wc, layer, tm):
    b, l, d = x.shape
    mrow = (lambda bi: bi) if mod_row is None else (lambda bi: mod_row)
    return pl.pallas_call(
        _outproj_kernel,
        grid=(b, l // tm),
        in_specs=[pl.BlockSpec((1, tm, attn.shape[2]), lambda bi, j: (bi, j, 0)),
                  pl.BlockSpec((1, hyt.shape[1], tm), lambda bi, j: (bi, 0, j)),
                  pl.BlockSpec((1, tm, cv.shape[2]), lambda bi, j: (bi, j, 0)),
                  pl.BlockSpec((1, tm, d), lambda bi, j: (bi, j, 0)),
                  pl.BlockSpec((1, 6, d), lambda bi, j: (mrow(bi), 0, 0)),
                  _layer_spec((1, d), layer),
                  _layer_spec(wa.shape[1:], layer),
                  _layer_spec(wh.shape[1:], layer),
                  _layer_spec(wc.shape[1:], layer)],
        out_specs=pl.BlockSpec((1, tm, d), lambda bi, j: (bi, j, 0)),
        out_shape=jax.ShapeDtypeStruct((b, l, d), F32),
        compiler_params=_cparams(2),
        name="outproj",
    )(attn, hyt, cv, x, mods, g_post, wa, wh, wc)


def _ffn_kernel(n_chunks, xp_ref, xc_ref, xn_ref, m_ref, gpre_ref, gpost_ref, wu_ref, wg_ref,
                dw_ref, db_ref, wd_ref, o_ref, gbuf):
    tm = xc_ref.shape[1]
    dff = wu_ref.shape[1]
    cw = dff // n_chunks
    j = pl.program_id(1)
    nj = pl.num_programs(1)
    shift, scale, gate = m_ref[0, 3:4, :], m_ref[0, 4:5, :], m_ref[0, 5:6, :]
    gpre = gpre_ref[...]

    def prep(x):
        return (_rms(x, gpre) * (1.0 + scale) + shift).astype(BF16)

    xc = xc_ref[0]
    hb = prep(xc)
    hext = jnp.concatenate([prep(xp_ref[0]), hb, prep(xn_ref[0])], axis=0)
    y = jnp.zeros((tm, o_ref.shape[2]), F32)
    for c in range(n_chunks):
        cols = slice(c * cw, (c + 1) * cw)
        u = jnp.dot(hb, wu_ref[:, cols], preferred_element_type=F32)
        g = jnp.dot(hext, wg_ref[:, cols], preferred_element_type=F32)
        gbuf[0:HALO] = jnp.where(j > 0, g[:HALO], 0.0)
        gbuf[HALO:HALO + tm] = g[HALO:HALO + tm]
        gbuf[HALO + tm:] = jnp.where(j < nj - 1, g[HALO + tm:], 0.0)
        conv = (dw_ref[0:1, cols] * gbuf[pl.ds(HALO - 1, tm), :]
                + dw_ref[1:2, cols] * gbuf[pl.ds(HALO, tm), :]
                + dw_ref[2:3, cols] * gbuf[pl.ds(HALO + 1, tm), :] + db_ref[:, cols])
        act = (_silu(conv) * u).astype(BF16)
        y = y + jnp.dot(act, wd_ref[cols, :], preferred_element_type=F32)
    o_ref[0] = xc + gate * _rms(y, gpost_ref[...])


def _ffn(x, mods, mod_row, g_pre, g_post, wu, wg, dw_w, dw_b, wd, layer, tm, n_chunks):
    b, l, d = x.shape
    dff = wu.shape[2]
    r = tm // HALO
    nh = l // HALO
    mrow = (lambda bi: bi) if mod_row is None else (lambda bi: mod_row)
    return pl.pallas_call(
        functools.partial(_ffn_kernel, n_chunks),
        grid=(b, l // tm),
        in_specs=[pl.BlockSpec((1, HALO, d), lambda bi, j: (bi, jnp.maximum(j * r - 1, 0), 0)),
                  pl.BlockSpec((1, tm, d), lambda bi, j: (bi, j, 0)),
                  pl.BlockSpec((1, HALO, d), lambda bi, j: (bi, jnp.minimum((j + 1) * r, nh - 1), 0)),
                  pl.BlockSpec((1, 6, d), lambda bi, j: (mrow(bi), 0, 0)),
                  _layer_spec((1, d), layer), _layer_spec((1, d), layer),
                  _layer_spec((d, dff), layer), _layer_spec((d, dff), layer),
                  _layer_spec((FFN_K, dff), layer), _layer_spec((1, dff), layer),
                  _layer_spec((dff, d), layer)],
        out_specs=pl.BlockSpec((1, tm, d), lambda bi, j: (bi, j, 0)),
        out_shape=jax.ShapeDtypeStruct((b, l, d), F32),
        scratch_shapes=[pltpu.VMEM((tm + 2 * HALO, dff // n_chunks), F32)],
        compiler_params=_cparams(2),
        name="conv_ffn",
    )(x, x, x, mods, g_pre, g_post, wu, wg, dw_w, dw_b, wd)


def _rope_tables(l):
    rows = l // GRID_W
    row = jnp.repeat(jnp.arange(rows, dtype=F32), GRID_W)
    col = jnp.tile(jnp.arange(GRID_W, dtype=F32), rows)
    inv = ROPE_THETA ** (-jnp.arange(0, AXIS_ROT, 2, dtype=F32) / AXIS_ROT)
    ar, ac = row[:, None] * inv, col[:, None] * inv
    cos = jnp.concatenate([jnp.cos(ar), jnp.cos(ar), jnp.cos(ac), jnp.cos(ac)], axis=1)
    sin = jnp.concatenate([-jnp.sin(ar), jnp.sin(ar), -jnp.sin(ac), jnp.sin(ac)], axis=1)
    rep = LANES // HEAD_DIM
    return jnp.tile(cos, (1, rep)), jnp.tile(sin, (1, rep))


def kernel(x, c, ctx, c_ctx, w_mod, b_mod, g_pre_mix, g_post_mix, g_pre_ffn, g_post_ffn, w_in, attn_sink, hy_short_w, hy_short_b, hy_w1, hy_b1, hy_freq1, hy_w2, hy_b2, hy_freq2, hy_w3, hy_bias, cv_dw_w, cv_dw_b, cv_ln_g, cv_ln_b, w_out, w_up, ffn_dw_w, ffn_dw_b, w_down):
    b, l, d = x.shape
    lc = ctx.shape[1]
    depth = w_in.shape[0]
    hy_ch = hy_bias.shape[2]
    cv_ch = cv_dw_w.shape[2]
    dff = w_down.shape[1]
    k0 = ATTN_W
    hy0 = k0 + 2 * KV_W
    cv0 = hy0 + (HY_ORDER + 1) * hy_ch
    fw = hy_w2.shape[1]

    v0 = k0 + KV_W
    wqkv = w_in[:, :, :v0].astype(BF16)
    whyt = jnp.swapaxes(w_in[:, :, v0:cv0], 1, 2).astype(BF16)
    wcv = w_in[:, :, cv0:].astype(BF16)
    wo_a = w_out[:, :ATTN_W].astype(BF16)
    wo_h = w_out[:, ATTN_W:ATTN_W + hy_ch].astype(BF16)
    wo_c = w_out[:, ATTN_W + hy_ch:].astype(BF16)
    wu = w_up[:, :, :dff].astype(BF16)
    wg = w_up[:, :, dff:].astype(BF16)
    wd = w_down.astype(BF16)
    row = lambda a: a[:, None, :]
    gpm, gqm, gpf, gqf = row(g_pre_mix), row(g_post_mix), row(g_pre_ffn), row(g_post_ffn)
    col = lambda a: a[:, :, None]
    w1t = jnp.swapaxes(jnp.pad(hy_w1, ((0, 0), (0, LANES - HY_EMB), (0, 0))), 1, 2)
    w3t = jnp.swapaxes(hy_w3, 1, 2).astype(BF16)
    deltas = jnp.abs(jnp.linspace(math.log(HY_TARGET) / HY_SLOW, math.log(HY_TARGET) / HY_FAST,
                                  hy_ch, dtype=F32))[:, None]
    short_b2 = row(hy_short_b)
    sw5 = jnp.transpose(hy_short_w.reshape(depth, 3, 3, hy_ch), (0, 2, 1, 3))[..., None]
    sb4 = hy_short_b.reshape(depth, 3, hy_ch)[..., None]
    hb4 = hy_bias[..., None]

    n_rows = 8 * ((b + 1 + 7) // 8)
    cond = jnp.zeros((n_rows, d), F32).at[:b].set(c).at[b].set(c_ctx)
    mods = _modulation(cond, w_mod, b_mod).reshape(depth, n_rows, 6, d)

    cs, sn = _rope_tables(l)
    tab_l, tab_c = _filter_tables(l), _filter_tables(lc)
    dc = _dft_consts(2 * l // LANES, LANES)
    dd = _dense_dft_consts(lc)
    tm = 512
    tmc = lc

    for i in range(depth):
        last = i == depth - 1
        m_i = mods[i]
        hyp = (w1t, col(hy_b1), col(hy_freq1), jnp.swapaxes(hy_w2, 1, 2), col(hy_b2), col(hy_freq2),
               w3t, deltas)

        q, k, v, hyt, cvu = _inproj(x, m_i, None, gpm, wqkv, whyt, wcv, i, cs, sn, tm)
        qc, kc, vc, hyt_c, cvu_c = _inproj(ctx, m_i, b, gpm, wqkv, whyt, wcv, i, None, None, tmc)

        attn = _window_attention(q, k, v, kc, vc, attn_sink[i][None, :], 256)
        taps = _hyena_taps(l, tab_l, *hyp, i)
        hy = _hyena_long(hyt, taps, hy_short_w[i], short_b2[i], hy_bias[i], dc, 8)
        cv = _conformer(cvu, cv_dw_w, row(cv_dw_b), row(cv_ln_g), row(cv_ln_b), i, tm)
        x = _outproj(attn, hy, cv, x, m_i, None, gqm, wo_a, wo_h, wo_c, i, tm)
        x = _ffn(x, m_i, None, gpf, gqf, wu, wg, ffn_dw_w, row(ffn_dw_b), wd, i, tm, 2)

        if not last:
            attn_c = _context_attention(qc, kc, vc, attn_sink[i][None, :])
            taps_c = _hyena_taps(lc, tab_c, *hyp, i)
            hy_c = _hyena_short(hyt_c, taps_c, sw5[i], sb4[i], hb4[i], dd, 32)
            cv_c = _conformer(cvu_c, cv_dw_w, row(cv_dw_b), row(cv_ln_g), row(cv_ln_b), i, tmc)
            ctx = _outproj(attn_c, hy_c, cv_c, ctx, m_i, b, gqm, wo_a, wo_h, wo_c, i, tmc)
            ctx = _ffn(ctx, m_i, b, gpf, gqf, wu, wg, ffn_dw_w, row(ffn_dw_b), wd, i, tmc, 2)
    return x
```

```python
import functools
import math

import numpy as np
import jax
import jax.numpy as jnp
from jax import lax
from jax.experimental import pallas as pl
from jax.experimental.pallas import tpu as pltpu

F32 = jnp.float32
BF16 = jnp.bfloat16
HIGHEST = lax.Precision.HIGHEST

N_HEADS = 8
N_KV_HEADS = 2
HEAD_DIM = 64
GROUP = N_HEADS // N_KV_HEADS
ATTN_W = N_HEADS * HEAD_DIM
KV_W = N_KV_HEADS * HEAD_DIM
WINDOW = 128
GRID_W = 64
ROPE_THETA = 10000.0
AXIS_ROT = HEAD_DIM // 2
HY_ORDER = 2
HY_EMB = 33
HY_BANDS = (HY_EMB - 1) // 2
HY_TARGET = 1e-2
HY_FAST = 0.3
HY_SLOW = 1.5
CV_K = 31
FFN_K = 3
EPS = 1e-6
NEG_INF = -1e30
LOG2E = math.log2(math.e)

LANES = 128
SUBLANES = 8
HALO = 16
VMEM_LIMIT = 56 * 1024 * 1024

NT = (((1,), (1,)), ((), ()))


def _cparams(n_axes):
    return pltpu.CompilerParams(dimension_semantics=("parallel",) * n_axes,
                                vmem_limit_bytes=VMEM_LIMIT)


def _full_spec(shape):
    nd = len(shape)
    return pl.BlockSpec(shape, lambda *_: (0,) * nd, pipeline_mode=pl.Buffered(1))


def _layer_spec(shape, layer):
    nd = len(shape)
    return pl.BlockSpec((None,) + tuple(shape), lambda *_: (layer,) + (0,) * nd,
                        pipeline_mode=pl.Buffered(1))


def _rms(t, g):
    return t * lax.rsqrt(jnp.mean(t * t, axis=-1, keepdims=True) + EPS) * g


def _silu(t):
    return t * jax.nn.sigmoid(t)


def _mod_kernel(c_ref, w_ref, b_ref, o_ref):
    s = _silu(c_ref[...])
    o_ref[0] = jnp.dot(s, w_ref[0], preferred_element_type=F32, precision=HIGHEST) + b_ref[0]


def _modulation(cond, w_mod, b_mod):
    depth, d, n = w_mod.shape
    rows = cond.shape[0]
    tn = 1536
    assert n % tn == 0
    return pl.pallas_call(
        _mod_kernel,
        grid=(depth, n // tn),
        in_specs=[pl.BlockSpec((rows, d), lambda i, j: (0, 0)),
                  pl.BlockSpec((1, d, tn), lambda i, j: (i, 0, j)),
                  pl.BlockSpec((1, 1, tn), lambda i, j: (i, 0, j))],
        out_specs=pl.BlockSpec((1, rows, tn), lambda i, j: (i, 0, j)),
        out_shape=jax.ShapeDtypeStruct((depth, rows, n), F32),
        compiler_params=_cparams(2),
        name="modulation",
    )(cond, w_mod, b_mod.reshape(depth, 1, n))


def _swap_halves(t):
    w = t.shape[-1]
    lane = lax.broadcasted_iota(jnp.int32, t.shape, 1)
    from_hi = pltpu.roll(t, w - AXIS_ROT // 2, axis=1)
    from_lo = pltpu.roll(t, AXIS_ROT // 2, axis=1)
    return jnp.where(lane % AXIS_ROT < AXIS_ROT // 2, from_hi, from_lo)


def _inproj_kernel(rope, x_ref, m_ref, g_ref, wqk_ref, wt_ref, wcv_ref, cs_ref, sn_ref,
                   q_ref, k_ref, vt_ref, hy_ref, cv_ref):
    x = x_ref[0]
    h = _rms(x, g_ref[...]) * (1.0 + m_ref[0, 1:2, :]) + m_ref[0, 0:1, :]
    hb = h.astype(BF16)
    p = jnp.dot(hb, wqk_ref[...], preferred_element_type=F32)
    pt = lax.dot_general(wt_ref[...], hb, NT, preferred_element_type=F32)
    q = p[:, :ATTN_W]
    k = p[:, ATTN_W:]
    if rope:
        cs = cs_ref[...]
        sn = sn_ref[...]
        cq = jnp.concatenate([cs] * (ATTN_W // LANES), axis=1)
        sq = jnp.concatenate([sn] * (ATTN_W // LANES), axis=1)
        q = q * cq + _swap_halves(q) * sq
        k = k * cs + _swap_halves(k) * sn
    q_ref[0] = (q * (HEAD_DIM ** -0.5 * LOG2E)).astype(BF16)
    k_ref[0] = jnp.concatenate([k, pltpu.roll(k, HEAD_DIM, axis=1)], axis=1).astype(BF16)
    vt_ref[0] = jnp.concatenate([pt[:KV_W], pt[HEAD_DIM:KV_W], pt[:HEAD_DIM]], axis=0).astype(BF16)
    hy_ref[0] = pt[KV_W:].astype(BF16)
    cv_ref[0] = jnp.dot(hb, wcv_ref[...], preferred_element_type=F32).astype(BF16)


def _inproj(x, mods, mod_row, g_pre, wqkv, whyt, wcv, layer, cs, sn, tm):
    b, l, d = x.shape
    rope = cs is not None
    if not rope:
        cs = jnp.zeros((l, LANES), F32)
        sn = cs
    nhy = whyt.shape[1] - KV_W
    ncv = wcv.shape[2]
    mrow = (lambda bi: bi) if mod_row is None else (lambda bi: mod_row)
    return pl.pallas_call(
        functools.partial(_inproj_kernel, rope),
        grid=(b, l // tm),
        in_specs=[pl.BlockSpec((1, tm, d), lambda bi, j: (bi, j, 0)),
                  pl.BlockSpec((1, 6, d), lambda bi, j: (mrow(bi), 0, 0)),
                  _layer_spec((1, d), layer),
                  _layer_spec(wqkv.shape[1:], layer),
                  _layer_spec(whyt.shape[1:], layer),
                  _layer_spec(wcv.shape[1:], layer),
                  pl.BlockSpec((tm, LANES), lambda bi, j: (j, 0)),
                  pl.BlockSpec((tm, LANES), lambda bi, j: (j, 0))],
        out_specs=[pl.BlockSpec((1, tm, ATTN_W), lambda bi, j: (bi, j, 0)),
                   pl.BlockSpec((1, tm, 2 * KV_W), lambda bi, j: (bi, j, 0)),
                   pl.BlockSpec((1, 2 * KV_W, tm), lambda bi, j: (bi, 0, j)),
                   pl.BlockSpec((1, nhy, tm), lambda bi, j: (bi, 0, j)),
                   pl.BlockSpec((1, tm, ncv), lambda bi, j: (bi, j, 0))],
        out_shape=[jax.ShapeDtypeStruct((b, l, ATTN_W), BF16),
                   jax.ShapeDtypeStruct((b, l, 2 * KV_W), BF16),
                   jax.ShapeDtypeStruct((b, 2 * KV_W, l), BF16),
                   jax.ShapeDtypeStruct((b, nhy, l), BF16),
                   jax.ShapeDtypeStruct((b, l, ncv), BF16)],
        compiler_params=_cparams(2),
        name="inproj",
    )(x, mods, g_pre, wqkv, whyt, wcv, cs, sn)


def _half_variants(a, axis):
    idx = lax.broadcasted_iota(jnp.int32, (LANES, 1) if axis == 0 else (1, LANES), axis)
    lo = idx < HEAD_DIM
    x, y = (a[:LANES], a[LANES:]) if axis == 0 else (a[:, :LANES], a[:, LANES:])
    zero = jnp.zeros_like(x)
    return {(0, 0): jnp.where(lo, x, zero), (0, 1): jnp.where(lo, zero, y),
            (1, 0): jnp.where(lo, y, zero), (1, 1): jnp.where(lo, zero, x)}


def _attend(jobs, q, sink_ref, o_ref):
    row_lo = lax.broadcasted_iota(jnp.int32, (LANES, 1), 0) < HEAD_DIM
    units = []
    for row0, rows, kvars, vtvars, masks in jobs:
        for kv in range(N_KV_HEADS):
            slabs = (2 * kv, 2 * kv + 1)
            qs = jnp.concatenate([q[row0:row0 + rows, s * LANES:(s + 1) * LANES] for s in slabs], axis=0)
            kcat = jnp.concatenate([kd[(kv, half)] for half in range(2) for kd in kvars], axis=0)
            s_all = lax.dot_general(kcat, qs, NT, preferred_element_type=F32)
            units.append((row0, rows, kv, slabs, kvars, vtvars, masks, s_all))
    for row0, rows, kv, slabs, kvars, vtvars, masks, s_all in units:
        vtcat = jnp.concatenate([vd[(kv, half)] for half in range(2) for vd in vtvars], axis=1)
        probs, denoms = [], []
        off = 0
        for half in range(2):
            sink = jnp.concatenate([jnp.full((1, rows), sink_ref[0, 2 * s + half] * LOG2E, F32)
                                    for s in slabs], axis=1)
            m = sink
            segs = []
            for kd, mk in zip(kvars, masks):
                n = kd[(kv, half)].shape[0]
                s = s_all[off:off + n]
                off += n
                if mk is not None:
                    s = jnp.where(mk, s, NEG_INF)
                segs.append(s)
                m = jnp.maximum(m, jnp.max(s, axis=0, keepdims=True))
            denom = jnp.exp2(sink - m)
            for s in segs:
                p = jnp.exp2(s - m)
                denom = denom + jnp.sum(p, axis=0, keepdims=True)
                probs.append(p.astype(BF16))
            denoms.append(denom)
        ot = jnp.dot(vtcat, jnp.concatenate(probs, axis=0), preferred_element_type=F32)
        ot = ot / jnp.where(row_lo, denoms[0], denoms[1])
        for t, s in enumerate(slabs):
            o_ref[0, row0:row0 + rows, s * LANES:(s + 1) * LANES] = (
                ot[:, t * rows:(t + 1) * rows].T.astype(o_ref.dtype))


def _win_attn_kernel(sink_ref, q_ref, kp_ref, kc_ref, kn_ref, vp_ref, vc_ref, vn_ref,
                     kx_ref, vx_ref, o_ref):
    tq = q_ref.shape[1]
    j = pl.program_id(1)
    nj = pl.num_programs(1)
    n_sub = tq // WINDOW
    span = 3 * WINDOW
    q = q_ref[0]
    kloc = _half_variants(jnp.concatenate([kp_ref[0], kc_ref[0], kn_ref[0]], axis=0), 1)
    vloc = _half_variants(jnp.concatenate([vp_ref[0], vc_ref[0], vn_ref[0]], axis=1), 0)
    kctx = _half_variants(kx_ref[0], 1)
    vctx = _half_variants(vx_ref[0], 0)
    c = lax.broadcasted_iota(jnp.int32, (span, 2 * WINDOW), 0)
    r = lax.broadcasted_iota(jnp.int32, (span, 2 * WINDOW), 1) % WINDOW
    band = (c >= r) & (c <= r + 2 * WINDOW)
    jobs = []
    for sb in range(n_sub):
        lo = jnp.where((j == 0) & (sb == 0), WINDOW, 0)
        hi = jnp.where((j == nj - 1) & (sb == n_sub - 1), 2 * WINDOW, span)
        mask = band & (c >= lo) & (c < hi)
        ks = {key: val[sb * WINDOW:sb * WINDOW + span] for key, val in kloc.items()}
        vs = {key: val[:, sb * WINDOW:sb * WINDOW + span] for key, val in vloc.items()}
        jobs.append((sb * WINDOW, WINDOW, [ks, kctx], [vs, vctx], [mask, None]))
    _attend(jobs, q, sink_ref, o_ref)


def _window_attention(q, k, v, k_ctx, v_ctx, sink, tq):
    b, l, _ = q.shape
    n_ctx = k_ctx.shape[1]
    assert tq % WINDOW == 0 and l % tq == 0
    r = tq // WINDOW
    nwb = l // WINDOW
    kvw = k.shape[2]
    i_prev = lambda j: jnp.maximum(j * r - 1, 0)
    i_next = lambda j: jnp.minimum((j + 1) * r, nwb - 1)
    prev = pl.BlockSpec((1, WINDOW, kvw), lambda bi, j: (bi, i_prev(j), 0))
    cur = pl.BlockSpec((1, tq, kvw), lambda bi, j: (bi, j, 0))
    nxt = pl.BlockSpec((1, WINDOW, kvw), lambda bi, j: (bi, i_next(j), 0))
    ctx = pl.BlockSpec((1, n_ctx, kvw), lambda bi, j: (bi, 0, 0))
    prev_t = pl.BlockSpec((1, kvw, WINDOW), lambda bi, j: (bi, 0, i_prev(j)))
    cur_t = pl.BlockSpec((1, kvw, tq), lambda bi, j: (bi, 0, j))
    nxt_t = pl.BlockSpec((1, kvw, WINDOW), lambda bi, j: (bi, 0, i_next(j)))
    ctx_t = pl.BlockSpec((1, kvw, n_ctx), lambda bi, j: (bi, 0, 0))
    return pl.pallas_call(
        _win_attn_kernel,
        grid=(b, l // tq),
        in_specs=[pl.BlockSpec(memory_space=pltpu.SMEM),
                  pl.BlockSpec((1, tq, ATTN_W), lambda bi, j: (bi, j, 0)),
                  prev, cur, nxt, prev_t, cur_t, nxt_t, ctx, ctx_t],
        out_specs=pl.BlockSpec((1, tq, ATTN_W), lambda bi, j: (bi, j, 0)),
        out_shape=jax.ShapeDtypeStruct((b, l, ATTN_W), BF16),
        compiler_params=_cparams(2),
        name="window_attention",
    )(sink, q, k, k, k, v, v, v, k_ctx, v_ctx)


def _ctx_attn_kernel(sink_ref, q_ref, k_ref, v_ref, o_ref):
    job = (0, q_ref.shape[1], [_half_variants(k_ref[0], 1)], [_half_variants(v_ref[0], 0)], [None])
    _attend([job], q_ref[0], sink_ref, o_ref)


def _context_attention(q, k, v, sink):
    b, l, _ = q.shape
    kvw = k.shape[2]
    return pl.pallas_call(
        _ctx_attn_kernel,
        grid=(b,),
        in_specs=[pl.BlockSpec(memory_space=pltpu.SMEM),
                  pl.BlockSpec((1, l, ATTN_W), lambda bi: (bi, 0, 0)),
                  pl.BlockSpec((1, l, kvw), lambda bi: (bi, 0, 0)),
                  pl.BlockSpec((1, kvw, l), lambda bi: (bi, 0, 0))],
        out_specs=pl.BlockSpec((1, l, ATTN_W), lambda bi: (bi, 0, 0)),
        out_shape=jax.ShapeDtypeStruct((b, l, ATTN_W), BF16),
        compiler_params=_cparams(1),
        name="context_attention",
    )(sink, q, k, v)


def _taps_kernel(seq_len, z_ref, w1_ref, b1_ref, f1_ref, w2_ref, b2_ref, f2_ref, w3t_ref,
                 dl_ref, o_ref):
    tn = z_ref.shape[1]
    ch = dl_ref.shape[0]
    z = z_ref[...]
    h = jnp.sin(f1_ref[...] * (jnp.dot(w1_ref[...], z, preferred_element_type=F32,
                                       precision=HIGHEST) + b1_ref[...]))
    h = jnp.sin(f2_ref[...] * (jnp.dot(w2_ref[...], h, preferred_element_type=F32,
                                       precision=HIGHEST) + b2_ref[...]))
    full = jnp.dot(w3t_ref[...], h.astype(BF16), preferred_element_type=F32)
    n0 = pl.program_id(0) * tn
    is_fwd = n0 < seq_len
    decay = jnp.exp(-z[0:1, :] * dl_ref[...])
    pos = n0 + lax.broadcasted_iota(jnp.int32, (1, tn), 1)
    for o in range(HY_ORDER):
        fwd = full[o * 2 * ch:o * 2 * ch + ch]
        bwd = full[o * 2 * ch + ch:(o + 1) * 2 * ch]
        taps = jnp.where(is_fwd, fwd, bwd) * decay
        o_ref[o] = jnp.where(pos == seq_len, 0.0, taps).astype(o_ref.dtype)


def _filter_tables(l):
    t = jnp.linspace(0.0, 1.0, l, dtype=F32)[:, None]
    w = (2.0 * math.pi / l) * jnp.arange(l, dtype=F32)[:, None]
    f = jnp.linspace(1e-4, HY_BANDS - 1, HY_BANDS, dtype=F32)[None, :]
    z = jnp.concatenate([t, jnp.cos(f * w), -jnp.sin(f * w)], axis=-1)
    zc = jnp.concatenate([z, z[-1:], jnp.flip(z[1:], axis=0)], axis=0)
    return jnp.pad(zc, ((0, 0), (0, LANES - HY_EMB))).T


def _hyena_taps(l, zct, w1t, b1, f1, w2t, b2, f2, w3t, deltas, layer):
    ch = deltas.shape[0]
    fw = w2t.shape[1]
    tn = min(1024, l)
    return pl.pallas_call(
        functools.partial(_taps_kernel, l),
        grid=(2 * l // tn,),
        in_specs=[pl.BlockSpec((LANES, tn), lambda j: (0, j)),
                  _layer_spec((fw, LANES), layer),
                  _layer_spec((fw, 1), layer),
                  _layer_spec((fw, 1), layer),
                  _layer_spec((fw, fw), layer),
                  _layer_spec((fw, 1), layer),
                  _layer_spec((fw, 1), layer),
                  _layer_spec(w3t.shape[1:], layer),
                  _full_spec((ch, 1))],
        out_specs=pl.BlockSpec((HY_ORDER, ch, tn), lambda j: (0, 0, j)),
        out_shape=jax.ShapeDtypeStruct((HY_ORDER, ch, 2 * l), BF16),
        compiler_params=_cparams(1),
        name="hyena_taps",
    )(zct, w1t, b1, f1, w2t, b2, f2, w3t, deltas)


def _dft_consts(na, nb):
    a = np.arange(na)
    fa = np.exp(-2j * np.pi * np.outer(a, a) / na)
    b = np.arange(nb)
    g = np.exp(-2j * np.pi * np.outer(b, b) / nb)
    fr, fi = fa.real, fa.imag
    gr, gi = g.real, g.imag
    h = na // 2
    m1 = np.block([[fr[:, :h], -fi[:, :h]], [fi[:, :h], fr[:, :h]]])
    m1f = np.concatenate([fr, fi], axis=0)
    m2 = np.block([[gr, gi], [-gi, gr]])
    m2i = np.block([[gr, -gi], [gi, gr]])
    m1i = np.block([[fr[:h], fi[:h]], [-fi[:h], fr[:h]]])
    tw = np.exp(-2j * np.pi * np.outer(a, b) / (na * nb))
    bf = lambda m: jnp.asarray(m, dtype=F32).astype(BF16)
    return dict(m1=bf(m1), m1f=bf(m1f), m2=bf(m2), m2i=bf(m2i), m1i=bf(m1i),
                tr=jnp.asarray(tw.real, F32), ti=jnp.asarray(tw.imag, F32))


def _stage2(a_all, tr, ti, m2):
    na, nb = tr.shape
    n = a_all.shape[1] // nb
    lhs = []
    for i in range(n):
        ar = a_all[:na, i * nb:(i + 1) * nb]
        ai = a_all[na:, i * nb:(i + 1) * nb]
        lhs.append(jnp.concatenate([ar * tr - ai * ti, ar * ti + ai * tr], axis=1).astype(BF16))
    return jnp.dot(jnp.concatenate(lhs, axis=0), m2, preferred_element_type=F32)


def _filter_spectra(taps, m1f, m2, tr, ti):
    na, nb = tr.shape
    a_all = jnp.dot(m1f, jnp.concatenate(taps, axis=1), preferred_element_type=F32)
    x = _stage2(a_all, tr, ti, m2) * (1.0 / (na * nb))
    return [x[i * na:(i + 1) * na] for i in range(len(taps))]


def _shift_prev(x):
    rows, nb = x.shape
    lane = lax.broadcasted_iota(jnp.int32, x.shape, 1)
    row = lax.broadcasted_iota(jnp.int32, x.shape, 0)
    r = pltpu.roll(x, 1, axis=1)
    r2 = pltpu.roll(r, 1, axis=0)
    return jnp.where(lane == 0, jnp.where(row == 0, 0.0, r2), r)


def _shift_next(x):
    rows, nb = x.shape
    lane = lax.broadcasted_iota(jnp.int32, x.shape, 1)
    row = lax.broadcasted_iota(jnp.int32, x.shape, 0)
    r = pltpu.roll(x, nb - 1, axis=1)
    r2 = pltpu.roll(r, rows - 1, axis=0)
    return jnp.where(lane == nb - 1, jnp.where(row == rows - 1, 0.0, r2), r)


def _fft_conv(zs, hfs, m1, m2, m2i, m1i, tr, ti):
    na, nb = tr.shape
    n = len(zs)
    h = na // 2
    zc = jnp.concatenate([jnp.concatenate([zr, zi], axis=0) for zr, zi in zs], axis=1)
    a_all = jnp.dot(m1, zc.astype(BF16), preferred_element_type=F32)
    x_all = _stage2(a_all, tr, ti, m2)
    ys = []
    for i in range(n):
        x = x_all[i * na:(i + 1) * na]
        xr, xi = x[:, :nb], x[:, nb:]
        hr, hi = hfs[i][:, :nb], hfs[i][:, nb:]
        ys.append(jnp.concatenate([xr * hr - xi * hi, xr * hi + xi * hr], axis=1).astype(BF16))
    b_all = jnp.dot(jnp.concatenate(ys, axis=0), m2i, preferred_element_type=F32)
    rhs = []
    for i in range(n):
        bm = b_all[i * na:(i + 1) * na]
        br, bi = bm[:, :nb], bm[:, nb:]
        rhs.append(jnp.concatenate([br * tr + bi * ti, bi * tr - br * ti], axis=0).astype(BF16))
    yo = jnp.dot(m1i, jnp.concatenate(rhs, axis=1), preferred_element_type=F32)
    return [(yo[:h, i * nb:(i + 1) * nb], yo[h:, i * nb:(i + 1) * nb]) for i in range(n)]


def _hyena_kernel(sw_ref, sb_ref, hb_ref, x1_ref, x2_ref, z_ref, taps_ref,
                  m1_ref, m1f_ref, m2_ref, m2i_ref, m1i_ref, tr_ref, ti_ref, o_ref):
    nbatch, ct = x1_ref.shape[:2]
    nch = hb_ref.shape[1]
    tr, ti = tr_ref[...], ti_ref[...]
    mats = (m1_ref[...], m2_ref[...], m2i_ref[...], m1i_ref[...], tr, ti)
    gates = (x1_ref, x2_ref)
    c0 = pl.program_id(0) * ct

    def short(ref, stream, b, c):
        ch = stream * nch + c0 + c
        x = ref[b, c].astype(F32)
        return (sw_ref[0, ch] * _shift_prev(x) + sw_ref[1, ch] * x
                + sw_ref[2, ch] * _shift_next(x) + sb_ref[0, ch])

    specs = _filter_spectra([taps_ref[o, c] for o in range(HY_ORDER) for c in range(ct)],
                            m1f_ref[...], m2_ref[...], tr, ti)
    seqs = [(c, p) for c in range(ct) for p in range(nbatch // 2)]
    zz = [(short(z_ref, 2, 2 * p, c), short(z_ref, 2, 2 * p + 1, c)) for c, p in seqs]
    for o in range(HY_ORDER):
        ys = _fft_conv(zz, [specs[o * ct + c] for c, _ in seqs], *mats)
        zz = [tuple(short(gates[o], o, 2 * p + q, c) * (ys[i][q] + hb_ref[o, c0 + c] * zz[i][q])
                    for q in range(2)) for i, (c, p) in enumerate(seqs)]
    for i, (c, p) in enumerate(seqs):
        for q in range(2):
            o_ref[2 * p + q, c] = zz[i][q].astype(o_ref.dtype)


def _hyena_long(hyt, taps, short_w, short_b, hy_bias, dc, ct):
    b, c3, l = hyt.shape
    ch = c3 // 3
    nb = LANES
    na = 2 * l // nb
    assert b % 2 == 0 and ct % 2 == 0 and ch % ct == 0
    h4 = hyt.reshape(b, c3, na // 2, nb)
    t4 = taps.reshape(HY_ORDER, ch, na, nb)
    blk = (b, ct, na // 2, nb)
    nct = ch // ct
    smem = pl.BlockSpec(memory_space=pltpu.SMEM)
    out = pl.pallas_call(
        _hyena_kernel,
        grid=(nct,),
        in_specs=[smem, smem, smem,
                  pl.BlockSpec(blk, lambda j: (0, j, 0, 0)),
                  pl.BlockSpec(blk, lambda j: (0, nct + j, 0, 0)),
                  pl.BlockSpec(blk, lambda j: (0, 2 * nct + j, 0, 0)),
                  pl.BlockSpec((HY_ORDER, ct, na, nb), lambda j: (0, j, 0, 0)),
                  _full_spec((2 * na, na)), _full_spec((2 * na, na)),
                  _full_spec((2 * nb, 2 * nb)), _full_spec((2 * nb, 2 * nb)),
                  _full_spec((na, 2 * na)), _full_spec((na, nb)), _full_spec((na, nb))],
        out_specs=pl.BlockSpec(blk, lambda j: (0, j, 0, 0)),
        out_shape=jax.ShapeDtypeStruct((b, ch, na // 2, nb), BF16),
        compiler_params=_cparams(1),
        name="hyena_long",
    )(short_w, short_b, hy_bias, h4, h4, h4, t4,
      dc["m1"], dc["m1f"], dc["m2"], dc["m2i"], dc["m1i"], dc["tr"], dc["ti"])
    return out.reshape(b, ch, l)


def _dense_dft_consts(l):
    n = 2 * l
    k = np.arange(n)
    ang = 2 * np.pi * np.outer(k, k) / n
    cos, sin = np.cos(ang), np.sin(ang)
    fwd = np.concatenate([cos, -sin], axis=1)
    inv = np.concatenate([cos[:, :l], -sin[:, :l]], axis=0) / n
    bf = lambda m: jnp.asarray(m, dtype=F32).astype(BF16)
    return dict(fwd_half=bf(fwd[:l]), fwd=bf(fwd), inv=bf(inv))


def _hyena_short_kernel(sw_ref, sb_ref, hb_ref, x1_ref, x2_ref, z_ref, taps_ref,
                        fh_ref, ff_ref, inv_ref, o_ref):
    nbatch, ct, l = x1_ref.shape
    n = 2 * l
    lane = lax.broadcasted_iota(jnp.int32, (ct, l), 1)

    def short(ref, stream, b):
        x = ref[b].astype(F32)
        prev = jnp.where(lane == 0, 0.0, pltpu.roll(x, 1, axis=1))
        nxt = jnp.where(lane == l - 1, 0.0, pltpu.roll(x, l - 1, axis=1))
        return sw_ref[stream, 0] * prev + sw_ref[stream, 1] * x + sw_ref[stream, 2] * nxt + sb_ref[stream]

    gates = (x1_ref, x2_ref)
    zz = [short(z_ref, 2, b) for b in range(nbatch)]
    for o in range(HY_ORDER):
        hf = jnp.dot(taps_ref[o].astype(BF16), ff_ref[...], preferred_element_type=F32)
        hr, hi = hf[:, :n], hf[:, n:]
        nxt = []
        for b in range(nbatch):
            x = jnp.dot(zz[b].astype(BF16), fh_ref[...], preferred_element_type=F32)
            xr, xi = x[:, :n], x[:, n:]
            y = jnp.concatenate([xr * hr - xi * hi, xr * hi + xi * hr], axis=1)
            conv = jnp.dot(y.astype(BF16), inv_ref[...], preferred_element_type=F32)
            nxt.append(short(gates[o], o, b) * (conv + hb_ref[o] * zz[b]))
        zz = nxt
    for b in range(nbatch):
        o_ref[b] = zz[b].astype(o_ref.dtype)


def _hyena_short(hyt, taps, short_w3, short_b3, hy_bias3, dd, ct):
    b, c3, l = hyt.shape
    ch = c3 // 3
    n = 2 * l
    nct = ch // ct
    blk = (b, ct, l)
    return pl.pallas_call(
        _hyena_short_kernel,
        grid=(nct,),
        in_specs=[pl.BlockSpec((3, 3, ct, 1), lambda j: (0, 0, j, 0)),
                  pl.BlockSpec((3, ct, 1), lambda j: (0, j, 0)),
                  pl.BlockSpec((HY_ORDER, ct, 1), lambda j: (0, j, 0)),
                  pl.BlockSpec(blk, lambda j: (0, j, 0)),
                  pl.BlockSpec(blk, lambda j: (0, nct + j, 0)),
                  pl.BlockSpec(blk, lambda j: (0, 2 * nct + j, 0)),
                  pl.BlockSpec((HY_ORDER, ct, n), lambda j: (0, j, 0)),
                  _full_spec((l, 2 * n)), _full_spec((n, 2 * n)), _full_spec((2 * n, l))],
        out_specs=pl.BlockSpec(blk, lambda j: (0, j, 0)),
        out_shape=jax.ShapeDtypeStruct((b, ch, l), BF16),
        compiler_params=_cparams(1),
        name="hyena_short",
    )(short_w3, short_b3, hy_bias3, hyt, hyt, hyt, taps, dd["fwd_half"], dd["fwd"], dd["inv"])


def _mixout_kernel(a_ref, hy_ref, up_ref, uc_ref, un_ref, x_ref, m_ref, g_ref, wa_ref, wh_ref, wc_ref,
                   dw_ref, db_ref, lg_ref, lb_ref, o_ref, buf, cv_buf):
    tm = uc_ref.shape[1]
    ch = cv_buf.shape[1]
    j = pl.program_id(1)
    nj = pl.num_programs(1)
    y = jnp.dot(a_ref[0], wa_ref[...], preferred_element_type=F32)
    y = y + pl.dot(hy_ref[0], wh_ref[...], trans_a=True)

    def glu(u):
        u = u.astype(F32)
        return u[:, :ch] * jax.nn.sigmoid(u[:, ch:])

    buf[0, 0:HALO] = jnp.where(j > 0, glu(up_ref[0]), 0.0)
    buf[0, HALO:HALO + tm] = glu(uc_ref[0])
    buf[0, HALO + tm:] = jnp.where(j < nj - 1, glu(un_ref[0]), 0.0)
    span = tm + 2 * HALO - SUBLANES
    for s in range(1, SUBLANES):
        buf[s, 0:span] = buf[0, pl.ds(s, span), :]
    pad = (CV_K - 1) // 2
    rc = 128
    for r0 in range(0, tm, rc):
        acc = jnp.zeros((rc, ch), F32) + db_ref[...]
        for k in range(CV_K):
            off = HALO - pad + k
            acc = acc + dw_ref[k:k + 1, :] * buf[off % SUBLANES, pl.ds(r0 + off - off % SUBLANES, rc), :]
        mu = jnp.mean(acc, axis=-1, keepdims=True)
        cen = acc - mu
        var = jnp.mean(cen * cen, axis=-1, keepdims=True)
        cv = cen * lax.rsqrt(var + EPS) * lg_ref[...] + lb_ref[...]
        cv_buf[r0:r0 + rc] = _silu(cv).astype(cv_buf.dtype)
    y = y + jnp.dot(cv_buf[...], wc_ref[...], preferred_element_type=F32)
    o_ref[0] = x_ref[0] + m_ref[0, 2:3, :] * _rms(y, g_ref[...])


def _mixout(attn, hyt, u, x, mods, mod_row, g_post, wa, wh, wc, dw_w, dw_b, ln_g, ln_b, layer, tm):
    b, l, d = x.shape
    c2 = u.shape[2]
    ch = c2 // 2
    r = tm // HALO
    nh = l // HALO
    mrow = (lambda bi: bi) if mod_row is None else (lambda bi: mod_row)
    return pl.pallas_call(
        _mixout_kernel,
        grid=(b, l // tm),
        in_specs=[pl.BlockSpec((1, tm, attn.shape[2]), lambda bi, j: (bi, j, 0)),
                  pl.BlockSpec((1, hyt.shape[1], tm), lambda bi, j: (bi, 0, j)),
                  pl.BlockSpec((1, HALO, c2), lambda bi, j: (bi, jnp.maximum(j * r - 1, 0), 0)),
                  pl.BlockSpec((1, tm, c2), lambda bi, j: (bi, j, 0)),
                  pl.BlockSpec((1, HALO, c2), lambda bi, j: (bi, jnp.minimum((j + 1) * r, nh - 1), 0)),
                  pl.BlockSpec((1, tm, d), lambda bi, j: (bi, j, 0)),
                  pl.BlockSpec((1, 6, d), lambda bi, j: (mrow(bi), 0, 0)),
                  _layer_spec((1, d), layer),
                  _layer_spec(wa.shape[1:], layer),
                  _layer_spec(wh.shape[1:], layer),
                  _layer_spec(wc.shape[1:], layer),
                  _layer_spec((CV_K, ch), layer),
                  _layer_spec((1, ch), layer), _layer_spec((1, ch), layer), _layer_spec((1, ch), layer)],
        out_specs=pl.BlockSpec((1, tm, d), lambda bi, j: (bi, j, 0)),
        out_shape=jax.ShapeDtypeStruct((b, l, d), F32),
        scratch_shapes=[pltpu.VMEM((SUBLANES, tm + 2 * HALO, ch), F32), pltpu.VMEM((tm, ch), BF16)],
        compiler_params=_cparams(2),
        name="mix_outproj",
    )(attn, hyt, u, u, u, x, mods, g_post, wa, wh, wc, dw_w, dw_b, ln_g, ln_b)


def _ffn_kernel(n_chunks, xp_ref, xc_ref, xn_ref, m_ref, gpre_ref, gpost_ref, wu_ref, wg_ref,
                dw_ref, db_ref, wd_ref, o_ref):
    tm = xc_ref.shape[1]
    dff = wu_ref.shape[1]
    cw = dff // n_chunks
    j = pl.program_id(1)
    nj = pl.num_programs(1)
    shift, scale, gate = m_ref[0, 3:4, :], m_ref[0, 4:5, :], m_ref[0, 5:6, :]
    gpre = gpre_ref[...]

    def prep(x):
        return (_rms(x, gpre) * (1.0 + scale) + shift).astype(BF16)

    xc = xc_ref[0]
    hb = prep(xc)
    hext = jnp.concatenate([prep(xp_ref[0]), hb, prep(xn_ref[0])], axis=0)
    y = jnp.zeros((tm, o_ref.shape[2]), F32)
    for c in range(n_chunks):
        cols = slice(c * cw, (c + 1) * cw)
        u = jnp.dot(hb, wu_ref[:, cols], preferred_element_type=F32)
        g = jnp.dot(hext, wg_ref[:, cols], preferred_element_type=F32)
        g = jnp.concatenate([jnp.where(j > 0, g[:HALO], 0.0), g[HALO:HALO + tm],
                             jnp.where(j < nj - 1, g[HALO + tm:], 0.0)], axis=0)
        rows = tm + 2 * HALO
        g_prev = pltpu.roll(g, 1, axis=0)[HALO:HALO + tm]
        g_next = pltpu.roll(g, rows - 1, axis=0)[HALO:HALO + tm]
        conv = (dw_ref[0:1, cols] * g_prev + dw_ref[1:2, cols] * g[HALO:HALO + tm]
                + dw_ref[2:3, cols] * g_next + db_ref[:, cols])
        act = (_silu(conv) * u).astype(BF16)
        y = y + jnp.dot(act, wd_ref[cols, :], preferred_element_type=F32)
    o_ref[0] = xc + gate * _rms(y, gpost_ref[...])


def _ffn(x, mods, mod_row, g_pre, g_post, wu, wg, dw_w, dw_b, wd, layer, tm, n_chunks):
    b, l, d = x.shape
    dff = wu.shape[2]
    r = tm // HALO
    nh = l // HALO
    mrow = (lambda bi: bi) if mod_row is None else (lambda bi: mod_row)
    return pl.pallas_call(
        functools.partial(_ffn_kernel, n_chunks),
        grid=(b, l // tm),
        in_specs=[pl.BlockSpec((1, HALO, d), lambda bi, j: (bi, jnp.maximum(j * r - 1, 0), 0)),
                  pl.BlockSpec((1, tm, d), lambda bi, j: (bi, j, 0)),
                  pl.BlockSpec((1, HALO, d), lambda bi, j: (bi, jnp.minimum((j + 1) * r, nh - 1), 0)),
                  pl.BlockSpec((1, 6, d), lambda bi, j: (mrow(bi), 0, 0)),
                  _layer_spec((1, d), layer), _layer_spec((1, d), layer),
                  _layer_spec((d, dff), layer), _layer_spec((d, dff), layer),
                  _layer_spec((FFN_K, dff), layer), _layer_spec((1, dff), layer),
                  _layer_spec((dff, d), layer)],
        out_specs=pl.BlockSpec((1, tm, d), lambda bi, j: (bi, j, 0)),
        out_shape=jax.ShapeDtypeStruct((b, l, d), F32),
        compiler_params=_cparams(2),
        name="conv_ffn",
    )(x, x, x, mods, g_pre, g_post, wu, wg, dw_w, dw_b, wd)


def _rope_tables(l):
    rows = l // GRID_W
    row = jnp.repeat(jnp.arange(rows, dtype=F32), GRID_W)
    col = jnp.tile(jnp.arange(GRID_W, dtype=F32), rows)
    inv = ROPE_THETA ** (-jnp.arange(0, AXIS_ROT, 2, dtype=F32) / AXIS_ROT)
    ar, ac = row[:, None] * inv, col[:, None] * inv
    cos = jnp.concatenate([jnp.cos(ar), jnp.cos(ar), jnp.cos(ac), jnp.cos(ac)], axis=1)
    sin = jnp.concatenate([-jnp.sin(ar), jnp.sin(ar), -jnp.sin(ac), jnp.sin(ac)], axis=1)
    rep = LANES // HEAD_DIM
    return jnp.tile(cos, (1, rep)), jnp.tile(sin, (1, rep))


def kernel(x, c, ctx, c_ctx, w_mod, b_mod, g_pre_mix, g_post_mix, g_pre_ffn, g_post_ffn, w_in, attn_sink, hy_short_w, hy_short_b, hy_w1, hy_b1, hy_freq1, hy_w2, hy_b2, hy_freq2, hy_w3, hy_bias, cv_dw_w, cv_dw_b, cv_ln_g, cv_ln_b, w_out, w_up, ffn_dw_w, ffn_dw_b, w_down):
    b, l, d = x.shape
    lc = ctx.shape[1]
    depth = w_in.shape[0]
    hy_ch = hy_bias.shape[2]
    cv_ch = cv_dw_w.shape[2]
    dff = w_down.shape[1]
    k0 = ATTN_W
    hy0 = k0 + 2 * KV_W
    cv0 = hy0 + (HY_ORDER + 1) * hy_ch
    fw = hy_w2.shape[1]

    v0 = k0 + KV_W
    wqkv = w_in[:, :, :v0].astype(BF16)
    whyt = jnp.swapaxes(w_in[:, :, v0:cv0], 1, 2).astype(BF16)
    wcv = w_in[:, :, cv0:].astype(BF16)
    wo_a = w_out[:, :ATTN_W].astype(BF16)
    wo_h = w_out[:, ATTN_W:ATTN_W + hy_ch].astype(BF16)
    wo_c = w_out[:, ATTN_W + hy_ch:].astype(BF16)
    wu = w_up[:, :, :dff].astype(BF16)
    wg = w_up[:, :, dff:].astype(BF16)
    wd = w_down.astype(BF16)
    row = lambda a: a[:, None, :]
    gpm, gqm, gpf, gqf = row(g_pre_mix), row(g_post_mix), row(g_pre_ffn), row(g_post_ffn)
    col = lambda a: a[:, :, None]
    w1t = jnp.swapaxes(jnp.pad(hy_w1, ((0, 0), (0, LANES - HY_EMB), (0, 0))), 1, 2)
    w3t = jnp.swapaxes(hy_w3, 1, 2).astype(BF16)
    deltas = jnp.abs(jnp.linspace(math.log(HY_TARGET) / HY_SLOW, math.log(HY_TARGET) / HY_FAST,
                                  hy_ch, dtype=F32))[:, None]
    short_b2 = row(hy_short_b)
    sw5 = jnp.transpose(hy_short_w.reshape(depth, 3, 3, hy_ch), (0, 2, 1, 3))[..., None]
    sb4 = hy_short_b.reshape(depth, 3, hy_ch)[..., None]
    hb4 = hy_bias[..., None]

    n_rows = 8 * ((b + 1 + 7) // 8)
    cond = jnp.zeros((n_rows, d), F32).at[:b].set(c).at[b].set(c_ctx)
    mods = _modulation(cond, w_mod, b_mod).reshape(depth, n_rows, 6, d)

    cs, sn = _rope_tables(l)
    tab_l, tab_c = _filter_tables(l), _filter_tables(lc)
    dc = _dft_consts(2 * l // LANES, LANES)
    dd = _dense_dft_consts(lc)
    tm = 512
    tmc = lc

    for i in range(depth):
        last = i == depth - 1
        m_i = mods[i]
        hyp = (w1t, col(hy_b1), col(hy_freq1), jnp.swapaxes(hy_w2, 1, 2), col(hy_b2), col(hy_freq2),
               w3t, deltas)

        q, k, v, hyt, cvu = _inproj(x, m_i, None, gpm, wqkv, whyt, wcv, i, cs, sn, tm)
        qc, kc, vc, hyt_c, cvu_c = _inproj(ctx, m_i, b, gpm, wqkv, whyt, wcv, i, None, None, tmc)

        attn = _window_attention(q, k, v, kc, vc, attn_sink[i][None, :], 256)
        taps = _hyena_taps(l, tab_l, *hyp, i)
        hy = _hyena_long(hyt, taps, hy_short_w[i], short_b2[i], hy_bias[i], dc, 8)
        cvp = (cv_dw_w, row(cv_dw_b), row(cv_ln_g), row(cv_ln_b))
        x = _mixout(attn, hy, cvu, x, m_i, None, gqm, wo_a, wo_h, wo_c, *cvp, i, tm)
        x = _ffn(x, m_i, None, gpf, gqf, wu, wg, ffn_dw_w, row(ffn_dw_b), wd, i, tm, 1)

        if not last:
            attn_c = _context_attention(qc, kc, vc, attn_sink[i][None, :])
            taps_c = _hyena_taps(lc, tab_c, *hyp, i)
            hy_c = _hyena_short(hyt_c, taps_c, sw5[i], sb4[i], hb4[i], dd, 32)
            ctx = _mixout(attn_c, hy_c, cvu_c, ctx, m_i, b, gqm, wo_a, wo_h, wo_c, *cvp, i, tmc)
            ctx = _ffn(ctx, m_i, b, gpf, gqf, wu, wg, ffn_dw_w, row(ffn_dw_b), wd, i, tmc, 1)
    return x
```

```python
import functools
import math

import numpy as np
import jax
import jax.numpy as jnp
from jax import lax
from jax.experimental import pallas as pl
from jax.experimental.pallas import tpu as pltpu

F32 = jnp.float32
BF16 = jnp.bfloat16
HIGHEST = lax.Precision.HIGHEST

N_HEADS = 8
N_KV_HEADS = 2
HEAD_DIM = 64
GROUP = N_HEADS // N_KV_HEADS
ATTN_W = N_HEADS * HEAD_DIM
KV_W = N_KV_HEADS * HEAD_DIM
WINDOW = 128
GRID_W = 64
ROPE_THETA = 10000.0
AXIS_ROT = HEAD_DIM // 2
HY_ORDER = 2
HY_EMB = 33
HY_BANDS = (HY_EMB - 1) // 2
HY_TARGET = 1e-2
HY_FAST = 0.3
HY_SLOW = 1.5
CV_K = 31
FFN_K = 3
EPS = 1e-6
NEG_INF = -1e30
LOG2E = math.log2(math.e)

LANES = 128
SUBLANES = 8
HALO = 16
ATTN_LOOKAHEAD = 2
VMEM_LIMIT = 56 * 1024 * 1024

NT = (((1,), (1,)), ((), ()))


def _cparams(n_axes):
    return pltpu.CompilerParams(dimension_semantics=("parallel",) * n_axes,
                                vmem_limit_bytes=VMEM_LIMIT)


def _full_spec(shape):
    nd = len(shape)
    return pl.BlockSpec(shape, lambda *_: (0,) * nd, pipeline_mode=pl.Buffered(1))


def _layer_spec(shape, layer):
    nd = len(shape)
    return pl.BlockSpec((None,) + tuple(shape), lambda *_: (layer,) + (0,) * nd,
                        pipeline_mode=pl.Buffered(1))


def _rms(t, g):
    return t * lax.rsqrt(jnp.mean(t * t, axis=-1, keepdims=True) + EPS) * g


def _silu(t):
    return t * jax.nn.sigmoid(t)


def _mod_kernel(c_ref, w_ref, b_ref, o_ref):
    s = _silu(c_ref[...])
    o_ref[0] = jnp.dot(s, w_ref[0], preferred_element_type=F32, precision=HIGHEST) + b_ref[0]


def _modulation(cond, w_mod, b_mod):
    depth, d, n = w_mod.shape
    rows = cond.shape[0]
    tn = 1536
    assert n % tn == 0
    return pl.pallas_call(
        _mod_kernel,
        grid=(depth, n // tn),
        in_specs=[pl.BlockSpec((rows, d), lambda i, j: (0, 0)),
                  pl.BlockSpec((1, d, tn), lambda i, j: (i, 0, j)),
                  pl.BlockSpec((1, 1, tn), lambda i, j: (i, 0, j))],
        out_specs=pl.BlockSpec((1, rows, tn), lambda i, j: (i, 0, j)),
        out_shape=jax.ShapeDtypeStruct((depth, rows, n), F32),
        compiler_params=_cparams(2),
        name="modulation",
    )(cond, w_mod, b_mod.reshape(depth, 1, n))


def _swap_halves(t):
    w = t.shape[-1]
    lane = lax.broadcasted_iota(jnp.int32, t.shape, 1)
    from_hi = pltpu.roll(t, w - AXIS_ROT // 2, axis=1)
    from_lo = pltpu.roll(t, AXIS_ROT // 2, axis=1)
    return jnp.where(lane % AXIS_ROT < AXIS_ROT // 2, from_hi, from_lo)


def _inproj_kernel(rope, x_ref, m_ref, g_ref, wqk_ref, wt_ref, wcv_ref, cs_ref, sn_ref,
                   q_ref, k_ref, vt_ref, hy_ref, cv_ref):
    x = x_ref[0]
    h = _rms(x, g_ref[...]) * (1.0 + m_ref[0, 1:2, :]) + m_ref[0, 0:1, :]
    hb = h.astype(BF16)
    p = jnp.dot(hb, wqk_ref[...], preferred_element_type=F32)
    pt = lax.dot_general(wt_ref[...], hb, NT, preferred_element_type=F32)
    q = p[:, :ATTN_W]
    k = p[:, ATTN_W:]
    if rope:
        cs = cs_ref[...]
        sn = sn_ref[...]
        cq = jnp.concatenate([cs] * (ATTN_W // LANES), axis=1)
        sq = jnp.concatenate([sn] * (ATTN_W // LANES), axis=1)
        q = q * cq + _swap_halves(q) * sq
        k = k * cs + _swap_halves(k) * sn
    q_ref[0] = (q * (HEAD_DIM ** -0.5 * LOG2E)).astype(BF16)
    k_ref[0] = jnp.concatenate([k, pltpu.roll(k, HEAD_DIM, axis=1)], axis=1).astype(BF16)
    vt_ref[0] = jnp.concatenate([pt[:KV_W], pt[HEAD_DIM:KV_W], pt[:HEAD_DIM]], axis=0).astype(BF16)
    hy_ref[0] = pt[KV_W:].astype(BF16)
    cv_ref[0] = jnp.dot(hb, wcv_ref[...], preferred_element_type=F32).astype(BF16)


def _inproj(x, mods, mod_row, g_pre, wqkv, whyt, wcv, layer, cs, sn, tm):
    b, l, d = x.shape
    rope = cs is not None
    if not rope:
        cs = jnp.zeros((l, LANES), F32)
        sn = cs
    nhy = whyt.shape[1] - KV_W
    ncv = wcv.shape[2]
    mrow = (lambda bi: bi) if mod_row is None else (lambda bi: mod_row)
    return pl.pallas_call(
        functools.partial(_inproj_kernel, rope),
        grid=(b, l // tm),
        in_specs=[pl.BlockSpec((1, tm, d), lambda bi, j: (bi, j, 0)),
                  pl.BlockSpec((1, 6, d), lambda bi, j: (mrow(bi), 0, 0)),
                  _layer_spec((1, d), layer),
                  _layer_spec(wqkv.shape[1:], layer),
                  _layer_spec(whyt.shape[1:], layer),
                  _layer_spec(wcv.shape[1:], layer),
                  pl.BlockSpec((tm, LANES), lambda bi, j: (j, 0)),
                  pl.BlockSpec((tm, LANES), lambda bi, j: (j, 0))],
        out_specs=[pl.BlockSpec((1, tm, ATTN_W), lambda bi, j: (bi, j, 0)),
                   pl.BlockSpec((1, tm, 2 * KV_W), lambda bi, j: (bi, j, 0)),
                   pl.BlockSpec((1, 2 * KV_W, tm), lambda bi, j: (bi, 0, j)),
                   pl.BlockSpec((1, nhy, tm), lambda bi, j: (bi, 0, j)),
                   pl.BlockSpec((1, tm, ncv), lambda bi, j: (bi, j, 0))],
        out_shape=[jax.ShapeDtypeStruct((b, l, ATTN_W), BF16),
                   jax.ShapeDtypeStruct((b, l, 2 * KV_W), BF16),
                   jax.ShapeDtypeStruct((b, 2 * KV_W, l), BF16),
                   jax.ShapeDtypeStruct((b, nhy, l), BF16),
                   jax.ShapeDtypeStruct((b, l, ncv), BF16)],
        compiler_params=_cparams(2),
        name="inproj",
    )(x, mods, g_pre, wqkv, whyt, wcv, cs, sn)


def _half_variants(a, axis):
    idx = lax.broadcasted_iota(jnp.int32, (LANES, 1) if axis == 0 else (1, LANES), axis)
    lo = idx < HEAD_DIM
    x, y = (a[:LANES], a[LANES:]) if axis == 0 else (a[:, :LANES], a[:, LANES:])
    zero = jnp.zeros_like(x)
    return {(0, 0): jnp.where(lo, x, zero), (0, 1): jnp.where(lo, zero, y),
            (1, 0): jnp.where(lo, y, zero), (1, 1): jnp.where(lo, zero, x)}


def _attend(jobs, q, sink_ref, o_ref):
    row_lo = lax.broadcasted_iota(jnp.int32, (LANES, 1), 0) < HEAD_DIM
    units = [(job, kv) for job in jobs for kv in range(N_KV_HEADS)]

    def scores(unit):
        (row0, rows, kvars, _, _), kv = unit
        qs = jnp.concatenate([q[row0:row0 + rows, s * LANES:(s + 1) * LANES]
                              for s in (2 * kv, 2 * kv + 1)], axis=0)
        kcat = jnp.concatenate([kd[(kv, half)] for half in range(2) for kd in kvars], axis=0)
        return lax.dot_general(kcat, qs, NT, preferred_element_type=F32)

    def finish(unit, s_all):
        (row0, rows, kvars, vtvars, masks), kv = unit
        slabs = (2 * kv, 2 * kv + 1)
        vtcat = jnp.concatenate([vd[(kv, half)] for half in range(2) for vd in vtvars], axis=1)
        probs, denoms = [], []
        off = 0
        for half in range(2):
            sink = jnp.concatenate([jnp.full((1, rows), sink_ref[0, 2 * s + half] * LOG2E, F32)
                                    for s in slabs], axis=1)
            m = sink
            segs = []
            for kd, mk in zip(kvars, masks):
                n = kd[(kv, half)].shape[0]
                s = s_all[off:off + n]
                off += n
                if mk is not None:
                    s = s + mk
                segs.append(s)
                m = jnp.maximum(m, jnp.max(s, axis=0, keepdims=True))
            denom = jnp.exp2(sink - m)
            for s in segs:
                p = jnp.exp2(s - m)
                denom = denom + jnp.sum(p, axis=0, keepdims=True)
                probs.append(p.astype(BF16))
            denoms.append(denom)
        ot = jnp.dot(vtcat, jnp.concatenate(probs, axis=0), preferred_element_type=F32)
        ot = ot / jnp.where(row_lo, denoms[0], denoms[1])
        for t, s in enumerate(slabs):
            o_ref[0, row0:row0 + rows, s * LANES:(s + 1) * LANES] = (
                ot[:, t * rows:(t + 1) * rows].T.astype(o_ref.dtype))

    pending = {}
    for i in range(len(units) + ATTN_LOOKAHEAD):
        if i < len(units):
            pending[i] = scores(units[i])
        if i >= ATTN_LOOKAHEAD:
            finish(units[i - ATTN_LOOKAHEAD], pending.pop(i - ATTN_LOOKAHEAD))


def _win_attn_kernel(sink_ref, q_ref, kp_ref, kc_ref, kn_ref, vp_ref, vc_ref, vn_ref,
                     kx_ref, vx_ref, bf_ref, bm_ref, bl_ref, o_ref):
    tq = q_ref.shape[1]
    n_sub = tq // WINDOW
    span = 3 * WINDOW
    q = q_ref[0]
    kloc = _half_variants(jnp.concatenate([kp_ref[0], kc_ref[0], kn_ref[0]], axis=0), 1)
    vloc = _half_variants(jnp.concatenate([vp_ref[0], vc_ref[0], vn_ref[0]], axis=1), 0)
    kctx = _half_variants(kx_ref[0], 1)
    vctx = _half_variants(vx_ref[0], 0)
    jobs = []
    for sb in range(n_sub):
        bias = (bf_ref if sb == 0 else bl_ref if sb == n_sub - 1 else bm_ref)[0]
        ks = {key: val[sb * WINDOW:sb * WINDOW + span] for key, val in kloc.items()}
        vs = {key: val[:, sb * WINDOW:sb * WINDOW + span] for key, val in vloc.items()}
        jobs.append((sb * WINDOW, WINDOW, [ks, kctx], [vs, vctx], [bias, None]))
    _attend(jobs, q, sink_ref, o_ref)


def _band_bias():
    span = 3 * WINDOW
    c = np.arange(span)[:, None]
    r = np.arange(2 * WINDOW)[None, :] % WINDOW
    band = (c >= r) & (c <= r + 2 * WINDOW)
    variants = [band & (c >= WINDOW), band, band & (c < 2 * WINDOW)]
    return jnp.asarray(np.where(np.stack(variants), 0.0, NEG_INF), F32)


def _window_attention(q, k, v, k_ctx, v_ctx, sink, tq):
    b, l, _ = q.shape
    n_ctx = k_ctx.shape[1]
    assert tq % WINDOW == 0 and l % tq == 0 and tq >= 2 * WINDOW
    r = tq // WINDOW
    nwb = l // WINDOW
    nj = l // tq
    kvw = k.shape[2]
    i_prev = lambda j: jnp.maximum(j * r - 1, 0)
    i_next = lambda j: jnp.minimum((j + 1) * r, nwb - 1)
    prev = pl.BlockSpec((1, WINDOW, kvw), lambda bi, j: (bi, i_prev(j), 0))
    cur = pl.BlockSpec((1, tq, kvw), lambda bi, j: (bi, j, 0))
    nxt = pl.BlockSpec((1, WINDOW, kvw), lambda bi, j: (bi, i_next(j), 0))
    ctx = pl.BlockSpec((1, n_ctx, kvw), lambda bi, j: (bi, 0, 0))
    prev_t = pl.BlockSpec((1, kvw, WINDOW), lambda bi, j: (bi, 0, i_prev(j)))
    cur_t = pl.BlockSpec((1, kvw, tq), lambda bi, j: (bi, 0, j))
    nxt_t = pl.BlockSpec((1, kvw, WINDOW), lambda bi, j: (bi, 0, i_next(j)))
    ctx_t = pl.BlockSpec((1, kvw, n_ctx), lambda bi, j: (bi, 0, 0))
    bshape = (1, 3 * WINDOW, 2 * WINDOW)
    bias_first = pl.BlockSpec(bshape, lambda bi, j: (jnp.where(j == 0, 0, 1), 0, 0))
    bias_mid = pl.BlockSpec(bshape, lambda bi, j: (1, 0, 0))
    bias_last = pl.BlockSpec(bshape, lambda bi, j: (jnp.where(j == nj - 1, 2, 1), 0, 0))
    bias = _band_bias()
    return pl.pallas_call(
        _win_attn_kernel,
        grid=(b, nj),
        in_specs=[pl.BlockSpec(memory_space=pltpu.SMEM),
                  pl.BlockSpec((1, tq, ATTN_W), lambda bi, j: (bi, j, 0)),
                  prev, cur, nxt, prev_t, cur_t, nxt_t, ctx, ctx_t, bias_first, bias_mid, bias_last],
        out_specs=pl.BlockSpec((1, tq, ATTN_W), lambda bi, j: (bi, j, 0)),
        out_shape=jax.ShapeDtypeStruct((b, l, ATTN_W), BF16),
        compiler_params=_cparams(2),
        name="window_attention",
    )(sink, q, k, k, k, v, v, v, k_ctx, v_ctx, bias, bias, bias)


def _ctx_attn_kernel(sink_ref, q_ref, k_ref, v_ref, o_ref):
    job = (0, q_ref.shape[1], [_half_variants(k_ref[0], 1)], [_half_variants(v_ref[0], 0)], [None])
    _attend([job], q_ref[0], sink_ref, o_ref)


def _context_attention(q, k, v, sink):
    b, l, _ = q.shape
    kvw = k.shape[2]
    return pl.pallas_call(
        _ctx_attn_kernel,
        grid=(b,),
        in_specs=[pl.BlockSpec(memory_space=pltpu.SMEM),
                  pl.BlockSpec((1, l, ATTN_W), lambda bi: (bi, 0, 0)),
                  pl.BlockSpec((1, l, kvw), lambda bi: (bi, 0, 0)),
                  pl.BlockSpec((1, kvw, l), lambda bi: (bi, 0, 0))],
        out_specs=pl.BlockSpec((1, l, ATTN_W), lambda bi: (bi, 0, 0)),
        out_shape=jax.ShapeDtypeStruct((b, l, ATTN_W), BF16),
        compiler_params=_cparams(1),
        name="context_attention",
    )(sink, q, k, v)


def _taps_kernel(seq_len, z_ref, w1_ref, b1_ref, f1_ref, w2_ref, b2_ref, f2_ref, w3t_ref,
                 dl_ref, o_ref):
    tn = z_ref.shape[1]
    ch = dl_ref.shape[0]
    z = z_ref[...]
    h = jnp.sin(f1_ref[...] * (jnp.dot(w1_ref[...], z, preferred_element_type=F32,
                                       precision=HIGHEST) + b1_ref[...]))
    h = jnp.sin(f2_ref[...] * (jnp.dot(w2_ref[...], h, preferred_element_type=F32,
                                       precision=HIGHEST) + b2_ref[...]))
    full = jnp.dot(w3t_ref[...], h.astype(BF16), preferred_element_type=F32)
    n0 = pl.program_id(0) * tn
    is_fwd = n0 < seq_len
    decay = jnp.exp(-z[0:1, :] * dl_ref[...])
    pos = n0 + lax.broadcasted_iota(jnp.int32, (1, tn), 1)
    for o in range(HY_ORDER):
        fwd = full[o * 2 * ch:o * 2 * ch + ch]
        bwd = full[o * 2 * ch + ch:(o + 1) * 2 * ch]
        taps = jnp.where(is_fwd, fwd, bwd) * decay
        o_ref[o] = jnp.where(pos == seq_len, 0.0, taps).astype(o_ref.dtype)


def _filter_tables(l):
    t = jnp.linspace(0.0, 1.0, l, dtype=F32)[:, None]
    w = (2.0 * math.pi / l) * jnp.arange(l, dtype=F32)[:, None]
    f = jnp.linspace(1e-4, HY_BANDS - 1, HY_BANDS, dtype=F32)[None, :]
    z = jnp.concatenate([t, jnp.cos(f * w), -jnp.sin(f * w)], axis=-1)
    zc = jnp.concatenate([z, z[-1:], jnp.flip(z[1:], axis=0)], axis=0)
    return jnp.pad(zc, ((0, 0), (0, LANES - HY_EMB))).T


def _hyena_taps(l, zct, w1t, b1, f1, w2t, b2, f2, w3t, deltas, layer):
    ch = deltas.shape[0]
    fw = w2t.shape[1]
    tn = min(1024, l)
    return pl.pallas_call(
        functools.partial(_taps_kernel, l),
        grid=(2 * l // tn,),
        in_specs=[pl.BlockSpec((LANES, tn), lambda j: (0, j)),
                  _layer_spec((fw, LANES), layer),
                  _layer_spec((fw, 1), layer),
                  _layer_spec((fw, 1), layer),
                  _layer_spec((fw, fw), layer),
                  _layer_spec((fw, 1), layer),
                  _layer_spec((fw, 1), layer),
                  _layer_spec(w3t.shape[1:], layer),
                  _full_spec((ch, 1))],
        out_specs=pl.BlockSpec((HY_ORDER, ch, tn), lambda j: (0, 0, j)),
        out_shape=jax.ShapeDtypeStruct((HY_ORDER, ch, 2 * l), BF16),
        compiler_params=_cparams(1),
        name="hyena_taps",
    )(zct, w1t, b1, f1, w2t, b2, f2, w3t, deltas)


def _dft_consts(na, nb):
    a = np.arange(na)
    fa = np.exp(-2j * np.pi * np.outer(a, a) / na)
    b = np.arange(nb)
    g = np.exp(-2j * np.pi * np.outer(b, b) / nb)
    fr, fi = fa.real, fa.imag
    gr, gi = g.real, g.imag
    h = na // 2
    m1 = np.block([[fr[:, :h], -fi[:, :h]], [fi[:, :h], fr[:, :h]]])
    m1f = np.concatenate([fr, fi], axis=0)
    m2 = np.block([[gr, gi], [-gi, gr]])
    m2i = np.block([[gr, -gi], [gi, gr]])
    m1i = np.block([[fr[:h], fi[:h]], [-fi[:h], fr[:h]]])
    tw = np.exp(-2j * np.pi * np.outer(a, b) / (na * nb))
    bf = lambda m: jnp.asarray(m, dtype=F32).astype(BF16)
    return dict(m1=bf(m1), m1f=bf(m1f), m2=bf(m2), m2i=bf(m2i), m1i=bf(m1i),
                tr=bf(tw.real), ti=bf(tw.imag))


def _stage2(a_all, tr, ti, m2):
    na, nb = tr.shape
    n = a_all.shape[1] // nb
    lhs = []
    for i in range(n):
        ar = a_all[:na, i * nb:(i + 1) * nb]
        ai = a_all[na:, i * nb:(i + 1) * nb]
        lhs.append(jnp.concatenate([ar * tr - ai * ti, ar * ti + ai * tr], axis=1))
    return jnp.dot(jnp.concatenate(lhs, axis=0), m2, preferred_element_type=F32)


def _filter_spectra(taps, m1f, m2, tr, ti):
    na, nb = tr.shape
    a_all = jnp.dot(m1f, jnp.concatenate(taps, axis=1), preferred_element_type=F32).astype(BF16)
    x = (_stage2(a_all, tr, ti, m2) * (1.0 / (na * nb))).astype(BF16)
    return [x[i * na:(i + 1) * na] for i in range(len(taps))]


def _shift_prev(x):
    rows, nb = x.shape
    lane = lax.broadcasted_iota(jnp.int32, x.shape, 1)
    row = lax.broadcasted_iota(jnp.int32, x.shape, 0)
    r = pltpu.roll(x, 1, axis=1)
    r2 = pltpu.roll(r, 1, axis=0)
    return jnp.where(lane == 0, jnp.where(row == 0, 0.0, r2), r)


def _shift_next(x):
    rows, nb = x.shape
    lane = lax.broadcasted_iota(jnp.int32, x.shape, 1)
    row = lax.broadcasted_iota(jnp.int32, x.shape, 0)
    r = pltpu.roll(x, nb - 1, axis=1)
    r2 = pltpu.roll(r, rows - 1, axis=0)
    return jnp.where(lane == nb - 1, jnp.where(row == rows - 1, 0.0, r2), r)


def _fft_conv(zs, hfs, m1, m2, m2i, m1i, tr, ti):
    na, nb = tr.shape
    n = len(zs)
    h = na // 2
    zc = jnp.concatenate([jnp.concatenate([zr, zi], axis=0) for zr, zi in zs], axis=1)
    a_all = jnp.dot(m1, zc.astype(BF16), preferred_element_type=F32).astype(BF16)
    x_all = _stage2(a_all, tr, ti, m2).astype(BF16)
    ys = []
    for i in range(n):
        x = x_all[i * na:(i + 1) * na]
        xr, xi = x[:, :nb], x[:, nb:]
        hr, hi = hfs[i][:, :nb], hfs[i][:, nb:]
        ys.append(jnp.concatenate([xr * hr - xi * hi, xr * hi + xi * hr], axis=1))
    b_all = jnp.dot(jnp.concatenate(ys, axis=0), m2i, preferred_element_type=F32).astype(BF16)
    rhs = []
    for i in range(n):
        bm = b_all[i * na:(i + 1) * na]
        br, bi = bm[:, :nb], bm[:, nb:]
        rhs.append(jnp.concatenate([br * tr + bi * ti, bi * tr - br * ti], axis=0))
    yo = jnp.dot(m1i, jnp.concatenate(rhs, axis=1), preferred_element_type=F32)
    return [(yo[:h, i * nb:(i + 1) * nb], yo[h:, i * nb:(i + 1) * nb]) for i in range(n)]


def _hyena_kernel(sw_ref, sb_ref, hb_ref, x1_ref, x2_ref, z_ref, taps_ref,
                  m1_ref, m1f_ref, m2_ref, m2i_ref, m1i_ref, tr_ref, ti_ref, o_ref):
    nbatch, ct = x1_ref.shape[:2]
    nch = hb_ref.shape[1]
    tr, ti = tr_ref[...], ti_ref[...]
    mats = (m1_ref[...], m2_ref[...], m2i_ref[...], m1i_ref[...], tr, ti)
    gates = (x1_ref, x2_ref)
    c0 = pl.program_id(0) * ct

    def short(ref, stream, b, c):
        ch = stream * nch + c0 + c
        x = ref[b, c].astype(F32)
        return (sw_ref[0, ch] * _shift_prev(x) + sw_ref[1, ch] * x
                + sw_ref[2, ch] * _shift_next(x) + sb_ref[0, ch])

    specs = _filter_spectra([taps_ref[o, c] for o in range(HY_ORDER) for c in range(ct)],
                            m1f_ref[...], m2_ref[...], tr, ti)
    seqs = [(c, p) for c in range(ct) for p in range(nbatch // 2)]
    zz = [(short(z_ref, 2, 2 * p, c), short(z_ref, 2, 2 * p + 1, c)) for c, p in seqs]
    for o in range(HY_ORDER):
        ys = _fft_conv(zz, [specs[o * ct + c] for c, _ in seqs], *mats)
        zz = [tuple(short(gates[o], o, 2 * p + q, c) * (ys[i][q] + hb_ref[o, c0 + c] * zz[i][q])
                    for q in range(2)) for i, (c, p) in enumerate(seqs)]
    for i, (c, p) in enumerate(seqs):
        for q in range(2):
            o_ref[2 * p + q, c] = zz[i][q].astype(o_ref.dtype)


def _hyena_long(hyt, taps, short_w, short_b, hy_bias, dc, ct):
    b, c3, l = hyt.shape
    ch = c3 // 3
    nb = LANES
    na = 2 * l // nb
    assert b % 2 == 0 and ct % 2 == 0 and ch % ct == 0
    h4 = hyt.reshape(b, c3, na // 2, nb)
    t4 = taps.reshape(HY_ORDER, ch, na, nb)
    blk = (b, ct, na // 2, nb)
    nct = ch // ct
    smem = pl.BlockSpec(memory_space=pltpu.SMEM)
    out = pl.pallas_call(
        _hyena_kernel,
        grid=(nct,),
        in_specs=[smem, smem, smem,
                  pl.BlockSpec(blk, lambda j: (0, j, 0, 0)),
                  pl.BlockSpec(blk, lambda j: (0, nct + j, 0, 0)),
                  pl.BlockSpec(blk, lambda j: (0, 2 * nct + j, 0, 0)),
                  pl.BlockSpec((HY_ORDER, ct, na, nb), lambda j: (0, j, 0, 0)),
                  _full_spec((2 * na, na)), _full_spec((2 * na, na)),
                  _full_spec((2 * nb, 2 * nb)), _full_spec((2 * nb, 2 * nb)),
                  _full_spec((na, 2 * na)), _full_spec((na, nb)), _full_spec((na, nb))],
        out_specs=pl.BlockSpec(blk, lambda j: (0, j, 0, 0)),
        out_shape=jax.ShapeDtypeStruct((b, ch, na // 2, nb), BF16),
        compiler_params=_cparams(1),
        name="hyena_long",
    )(short_w, short_b, hy_bias, h4, h4, h4, t4,
      dc["m1"], dc["m1f"], dc["m2"], dc["m2i"], dc["m1i"], dc["tr"], dc["ti"])
    return out.reshape(b, ch, l)


def _dense_dft_consts(l):
    n = 2 * l
    k = np.arange(n)
    ang = 2 * np.pi * np.outer(k, k) / n
    cos, sin = np.cos(ang), np.sin(ang)
    fwd = np.concatenate([cos, -sin], axis=1)
    inv = np.concatenate([cos[:, :l], -sin[:, :l]], axis=0) / n
    bf = lambda m: jnp.asarray(m, dtype=F32).astype(BF16)
    return dict(fwd_half=bf(fwd[:l]), fwd=bf(fwd), inv=bf(inv))


def _hyena_short_kernel(sw_ref, sb_ref, hb_ref, x1_ref, x2_ref, z_ref, taps_ref,
                        fh_ref, ff_ref, inv_ref, o_ref):
    nbatch, ct, l = x1_ref.shape
    n = 2 * l
    lane = lax.broadcasted_iota(jnp.int32, (ct, l), 1)

    def short(ref, stream, b):
        x = ref[b].astype(F32)
        prev = jnp.where(lane == 0, 0.0, pltpu.roll(x, 1, axis=1))
        nxt = jnp.where(lane == l - 1, 0.0, pltpu.roll(x, l - 1, axis=1))
        return sw_ref[stream, 0] * prev + sw_ref[stream, 1] * x + sw_ref[stream, 2] * nxt + sb_ref[stream]

    gates = (x1_ref, x2_ref)
    zz = [short(z_ref, 2, b) for b in range(nbatch)]
    for o in range(HY_ORDER):
        hf = jnp.dot(taps_ref[o].astype(BF16), ff_ref[...], preferred_element_type=F32)
        hr, hi = hf[:, :n], hf[:, n:]
        nxt = []
        for b in range(nbatch):
            x = jnp.dot(zz[b].astype(BF16), fh_ref[...], preferred_element_type=F32)
            xr, xi = x[:, :n], x[:, n:]
            y = jnp.concatenate([xr * hr - xi * hi, xr * hi + xi * hr], axis=1)
            conv = jnp.dot(y.astype(BF16), inv_ref[...], preferred_element_type=F32)
            nxt.append(short(gates[o], o, b) * (conv + hb_ref[o] * zz[b]))
        zz = nxt
    for b in range(nbatch):
        o_ref[b] = zz[b].astype(o_ref.dtype)


def _hyena_short(hyt, taps, short_w3, short_b3, hy_bias3, dd, ct):
    b, c3, l = hyt.shape
    ch = c3 // 3
    n = 2 * l
    nct = ch // ct
    blk = (b, ct, l)
    return pl.pallas_call(
        _hyena_short_kernel,
        grid=(nct,),
        in_specs=[pl.BlockSpec((3, 3, ct, 1), lambda j: (0, 0, j, 0)),
                  pl.BlockSpec((3, ct, 1), lambda j: (0, j, 0)),
                  pl.BlockSpec((HY_ORDER, ct, 1), lambda j: (0, j, 0)),
                  pl.BlockSpec(blk, lambda j: (0, j, 0)),
                  pl.BlockSpec(blk, lambda j: (0, nct + j, 0)),
                  pl.BlockSpec(blk, lambda j: (0, 2 * nct + j, 0)),
                  pl.BlockSpec((HY_ORDER, ct, n), lambda j: (0, j, 0)),
                  _full_spec((l, 2 * n)), _full_spec((n, 2 * n)), _full_spec((2 * n, l))],
        out_specs=pl.BlockSpec(blk, lambda j: (0, j, 0)),
        out_shape=jax.ShapeDtypeStruct((b, ch, l), BF16),
        compiler_params=_cparams(1),
        name="hyena_short",
    )(short_w3, short_b3, hy_bias3, hyt, hyt, hyt, taps, dd["fwd_half"], dd["fwd"], dd["inv"])


def _mixout_kernel(a_ref, hy_ref, up_ref, uc_ref, un_ref, x_ref, m_ref, g_ref, wa_ref, wh_ref, wc_ref,
                   dw_ref, db_ref, lg_ref, lb_ref, o_ref, buf, cv_buf):
    tm = uc_ref.shape[1]
    ch = cv_buf.shape[1]
    j = pl.program_id(1)
    nj = pl.num_programs(1)
    y = jnp.dot(a_ref[0], wa_ref[...], preferred_element_type=F32)
    y = y + pl.dot(hy_ref[0], wh_ref[...], trans_a=True)

    def glu(u):
        u = u.astype(F32)
        return u[:, :ch] * jax.nn.sigmoid(u[:, ch:])

    buf[0, 0:HALO] = jnp.where(j > 0, glu(up_ref[0]), 0.0)
    buf[0, HALO:HALO + tm] = glu(uc_ref[0])
    buf[0, HALO + tm:] = jnp.where(j < nj - 1, glu(un_ref[0]), 0.0)
    span = tm + 2 * HALO - SUBLANES
    for s in range(1, SUBLANES):
        buf[s, 0:span] = buf[0, pl.ds(s, span), :]
    pad = (CV_K - 1) // 2
    rc = 128
    for r0 in range(0, tm, rc):
        acc = jnp.zeros((rc, ch), F32) + db_ref[...]
        for k in range(CV_K):
            off = HALO - pad + k
            acc = acc + dw_ref[k:k + 1, :] * buf[off % SUBLANES, pl.ds(r0 + off - off % SUBLANES, rc), :]
        mu = jnp.mean(acc, axis=-1, keepdims=True)
        cen = acc - mu
        var = jnp.mean(cen * cen, axis=-1, keepdims=True)
        cv = cen * lax.rsqrt(var + EPS) * lg_ref[...] + lb_ref[...]
        cv_buf[r0:r0 + rc] = _silu(cv).astype(cv_buf.dtype)
    y = y + jnp.dot(cv_buf[...], wc_ref[...], preferred_element_type=F32)
    o_ref[0] = x_ref[0] + m_ref[0, 2:3, :] * _rms(y, g_ref[...])


def _mixout(attn, hyt, u, x, mods, mod_row, g_post, wa, wh, wc, dw_w, dw_b, ln_g, ln_b, layer, tm):
    b, l, d = x.shape
    c2 = u.shape[2]
    ch = c2 // 2
    r = tm // HALO
    nh = l // HALO
    mrow = (lambda bi: bi) if mod_row is None else (lambda bi: mod_row)
    return pl.pallas_call(
        _mixout_kernel,
        grid=(b, l // tm),
        in_specs=[pl.BlockSpec((1, tm, attn.shape[2]), lambda bi, j: (bi, j, 0)),
                  pl.BlockSpec((1, hyt.shape[1], tm), lambda bi, j: (bi, 0, j)),
                  pl.BlockSpec((1, HALO, c2), lambda bi, j: (bi, jnp.maximum(j * r - 1, 0), 0)),
                  pl.BlockSpec((1, tm, c2), lambda bi, j: (bi, j, 0)),
                  pl.BlockSpec((1, HALO, c2), lambda bi, j: (bi, jnp.minimum((j + 1) * r, nh - 1), 0)),
                  pl.BlockSpec((1, tm, d), lambda bi, j: (bi, j, 0)),
                  pl.BlockSpec((1, 6, d), lambda bi, j: (mrow(bi), 0, 0)),
                  _layer_spec((1, d), layer),
                  _layer_spec(wa.shape[1:], layer),
                  _layer_spec(wh.shape[1:], layer),
                  _layer_spec(wc.shape[1:], layer),
                  _layer_spec((CV_K, ch), layer),
                  _layer_spec((1, ch), layer), _layer_spec((1, ch), layer), _layer_spec((1, ch), layer)],
        out_specs=pl.BlockSpec((1, tm, d), lambda bi, j: (bi, j, 0)),
        out_shape=jax.ShapeDtypeStruct((b, l, d), F32),
        scratch_shapes=[pltpu.VMEM((SUBLANES, tm + 2 * HALO, ch), F32), pltpu.VMEM((tm, ch), BF16)],
        compiler_params=_cparams(2),
        name="mix_outproj",
    )(attn, hyt, u, u, u, x, mods, g_post, wa, wh, wc, dw_w, dw_b, ln_g, ln_b)


def _ffn_kernel(n_chunks, xp_ref, xc_ref, xn_ref, m_ref, gpre_ref, gpost_ref, wu_ref, wg_ref,
                dw_ref, db_ref, wd_ref, o_ref):
    tm = xc_ref.shape[1]
    dff = wu_ref.shape[1]
    cw = dff // n_chunks
    j = pl.program_id(1)
    nj = pl.num_programs(1)
    shift, scale, gate = m_ref[0, 3:4, :], m_ref[0, 4:5, :], m_ref[0, 5:6, :]
    gpre = gpre_ref[...]

    def prep(x):
        return (_rms(x, gpre) * (1.0 + scale) + shift).astype(BF16)

    xc = xc_ref[0]
    hb = prep(xc)
    hext = jnp.concatenate([prep(xp_ref[0]), hb, prep(xn_ref[0])], axis=0)
    y = jnp.zeros((tm, o_ref.shape[2]), F32)
    for c in range(n_chunks):
        cols = slice(c * cw, (c + 1) * cw)
        u = jnp.dot(hb, wu_ref[:, cols], preferred_element_type=F32)
        g = jnp.dot(hext, wg_ref[:, cols], preferred_element_type=F32)
        g = jnp.concatenate([jnp.where(j > 0, g[:HALO], 0.0), g[HALO:HALO + tm],
                             jnp.where(j < nj - 1, g[HALO + tm:], 0.0)], axis=0)
        rows = tm + 2 * HALO
        g_prev = pltpu.roll(g, 1, axis=0)[HALO:HALO + tm]
        g_next = pltpu.roll(g, rows - 1, axis=0)[HALO:HALO + tm]
        conv = (dw_ref[0:1, cols] * g_prev + dw_ref[1:2, cols] * g[HALO:HALO + tm]
                + dw_ref[2:3, cols] * g_next + db_ref[:, cols])
        act = (_silu(conv) * u).astype(BF16)
        y = y + jnp.dot(act, wd_ref[cols, :], preferred_element_type=F32)
    o_ref[0] = xc + gate * _rms(y, gpost_ref[...])


def _ffn(x, mods, mod_row, g_pre, g_post, wu, wg, dw_w, dw_b, wd, layer, tm, n_chunks):
    b, l, d = x.shape
    dff = wu.shape[2]
    r = tm // HALO
    nh = l // HALO
    mrow = (lambda bi: bi) if mod_row is None else (lambda bi: mod_row)
    return pl.pallas_call(
        functools.partial(_ffn_kernel, n_chunks),
        grid=(b, l // tm),
        in_specs=[pl.BlockSpec((1, HALO, d), lambda bi, j: (bi, jnp.maximum(j * r - 1, 0), 0)),
                  pl.BlockSpec((1, tm, d), lambda bi, j: (bi, j, 0)),
                  pl.BlockSpec((1, HALO, d), lambda bi, j: (bi, jnp.minimum((j + 1) * r, nh - 1), 0)),
                  pl.BlockSpec((1, 6, d), lambda bi, j: (mrow(bi), 0, 0)),
                  _layer_spec((1, d), layer), _layer_spec((1, d), layer),
                  _layer_spec((d, dff), layer), _layer_spec((d, dff), layer),
                  _layer_spec((FFN_K, dff), layer), _layer_spec((1, dff), layer),
                  _layer_spec((dff, d), layer)],
        out_specs=pl.BlockSpec((1, tm, d), lambda bi, j: (bi, j, 0)),
        out_shape=jax.ShapeDtypeStruct((b, l, d), F32),
        compiler_params=_cparams(2),
        name="conv_ffn",
    )(x, x, x, mods, g_pre, g_post, wu, wg, dw_w, dw_b, wd)


def _rope_tables(l):
    rows = l // GRID_W
    row = jnp.repeat(jnp.arange(rows, dtype=F32), GRID_W)
    col = jnp.tile(jnp.arange(GRID_W, dtype=F32), rows)
    inv = ROPE_THETA ** (-jnp.arange(0, AXIS_ROT, 2, dtype=F32) / AXIS_ROT)
    ar, ac = row[:, None] * inv, col[:, None] * inv
    cos = jnp.concatenate([jnp.cos(ar), jnp.cos(ar), jnp.cos(ac), jnp.cos(ac)], axis=1)
    sin = jnp.concatenate([-jnp.sin(ar), jnp.sin(ar), -jnp.sin(ac), jnp.sin(ac)], axis=1)
    rep = LANES // HEAD_DIM
    return jnp.tile(cos, (1, rep)), jnp.tile(sin, (1, rep))


def kernel(x, c, ctx, c_ctx, w_mod, b_mod, g_pre_mix, g_post_mix, g_pre_ffn, g_post_ffn, w_in, attn_sink, hy_short_w, hy_short_b, hy_w1, hy_b1, hy_freq1, hy_w2, hy_b2, hy_freq2, hy_w3, hy_bias, cv_dw_w, cv_dw_b, cv_ln_g, cv_ln_b, w_out, w_up, ffn_dw_w, ffn_dw_b, w_down):
    b, l, d = x.shape
    lc = ctx.shape[1]
    depth = w_in.shape[0]
    hy_ch = hy_bias.shape[2]
    cv_ch = cv_dw_w.shape[2]
    dff = w_down.shape[1]
    k0 = ATTN_W
    hy0 = k0 + 2 * KV_W
    cv0 = hy0 + (HY_ORDER + 1) * hy_ch
    fw = hy_w2.shape[1]

    v0 = k0 + KV_W
    wqkv = w_in[:, :, :v0].astype(BF16)
    whyt = jnp.swapaxes(w_in[:, :, v0:cv0], 1, 2).astype(BF16)
    wcv = w_in[:, :, cv0:].astype(BF16)
    wo_a = w_out[:, :ATTN_W].astype(BF16)
    wo_h = w_out[:, ATTN_W:ATTN_W + hy_ch].astype(BF16)
    wo_c = w_out[:, ATTN_W + hy_ch:].astype(BF16)
    wu = w_up[:, :, :dff].astype(BF16)
    wg = w_up[:, :, dff:].astype(BF16)
    wd = w_down.astype(BF16)
    row = lambda a: a[:, None, :]
    gpm, gqm, gpf, gqf = row(g_pre_mix), row(g_post_mix), row(g_pre_ffn), row(g_post_ffn)
    col = lambda a: a[:, :, None]
    w1t = jnp.swapaxes(jnp.pad(hy_w1, ((0, 0), (0, LANES - HY_EMB), (0, 0))), 1, 2)
    w3t = jnp.swapaxes(hy_w3, 1, 2).astype(BF16)
    deltas = jnp.abs(jnp.linspace(math.log(HY_TARGET) / HY_SLOW, math.log(HY_TARGET) / HY_FAST,
                                  hy_ch, dtype=F32))[:, None]
    short_b2 = row(hy_short_b)
    sw5 = jnp.transpose(hy_short_w.reshape(depth, 3, 3, hy_ch), (0, 2, 1, 3))[..., None]
    sb4 = hy_short_b.reshape(depth, 3, hy_ch)[..., None]
    hb4 = hy_bias[..., None]

    n_rows = 8 * ((b + 1 + 7) // 8)
    cond = jnp.zeros((n_rows, d), F32).at[:b].set(c).at[b].set(c_ctx)
    mods = _modulation(cond, w_mod, b_mod).reshape(depth, n_rows, 6, d)

    cs, sn = _rope_tables(l)
    tab_l, tab_c = _filter_tables(l), _filter_tables(lc)
    dc = _dft_consts(2 * l // LANES, LANES)
    dd = _dense_dft_consts(lc)
    tm_in, tm_mix, tm_ffn = min(512, l), min(512, l), min(512, l)
    tmc = lc

    for i in range(depth):
        last = i == depth - 1
        m_i = mods[i]
        hyp = (w1t, col(hy_b1), col(hy_freq1), jnp.swapaxes(hy_w2, 1, 2), col(hy_b2), col(hy_freq2),
               w3t, deltas)

        q, k, v, hyt, cvu = _inproj(x, m_i, None, gpm, wqkv, whyt, wcv, i, cs, sn, tm_in)
        qc, kc, vc, hyt_c, cvu_c = _inproj(ctx, m_i, b, gpm, wqkv, whyt, wcv, i, None, None, tmc)

        attn = _window_attention(q, k, v, kc, vc, attn_sink[i][None, :], min(1024, l))
        taps = _hyena_taps(l, tab_l, *hyp, i)
        hy = _hyena_long(hyt, taps, hy_short_w[i], short_b2[i], hy_bias[i], dc, 8)
        cvp = (cv_dw_w, row(cv_dw_b), row(cv_ln_g), row(cv_ln_b))
        x = _mixout(attn, hy, cvu, x, m_i, None, gqm, wo_a, wo_h, wo_c, *cvp, i, tm_mix)
        x = _ffn(x, m_i, None, gpf, gqf, wu, wg, ffn_dw_w, row(ffn_dw_b), wd, i, tm_ffn, 1)

        if not last:
            attn_c = _context_attention(qc, kc, vc, attn_sink[i][None, :])
            taps_c = _hyena_taps(lc, tab_c, *hyp, i)
            hy_c = _hyena_short(hyt_c, taps_c, sw5[i], sb4[i], hb4[i], dd, 32)
            ctx = _mixout(attn_c, hy_c, cvu_c, ctx, m_i, b, gqm, wo_a, wo_h, wo_c, *cvp, i, tmc)
            ctx = _ffn(ctx, m_i, b, gpf, gqf, wu, wg, ffn_dw_w, row(ffn_dw_b), wd, i, tmc, 1)
    return x
```

```python
import functools
import math

import numpy as np
import jax
import jax.numpy as jnp
from jax import lax
from jax.experimental import pallas as pl
from jax.experimental.pallas import tpu as pltpu

F32 = jnp.float32
BF16 = jnp.bfloat16
HIGHEST = lax.Precision.HIGHEST

N_HEADS = 8
N_KV_HEADS = 2
HEAD_DIM = 64
GROUP = N_HEADS // N_KV_HEADS
ATTN_W = N_HEADS * HEAD_DIM
KV_W = N_KV_HEADS * HEAD_DIM
WINDOW = 128
GRID_W = 64
ROPE_THETA = 10000.0
AXIS_ROT = HEAD_DIM // 2
HY_ORDER = 2
HY_EMB = 33
HY_BANDS = (HY_EMB - 1) // 2
HY_TARGET = 1e-2
HY_FAST = 0.3
HY_SLOW = 1.5
CV_K = 31
FFN_K = 3
EPS = 1e-6
NEG_INF = -1e30
LOG2E = math.log2(math.e)

LANES = 128
SUBLANES = 8
HALO = 16
ATTN_LOOKAHEAD = 2
VMEM_LIMIT = 56 * 1024 * 1024

NT = (((1,), (1,)), ((), ()))


def _cparams(n_axes):
    return pltpu.CompilerParams(dimension_semantics=("parallel",) * n_axes,
                                vmem_limit_bytes=VMEM_LIMIT)


def _full_spec(shape):
    nd = len(shape)
    return pl.BlockSpec(shape, lambda *_: (0,) * nd, pipeline_mode=pl.Buffered(1))


def _layer_spec(shape, layer):
    nd = len(shape)
    return pl.BlockSpec((None,) + tuple(shape), lambda *_: (layer,) + (0,) * nd,
                        pipeline_mode=pl.Buffered(1))


def _rms(t, g):
    return t * lax.rsqrt(jnp.mean(t * t, axis=-1, keepdims=True) + EPS) * g


def _silu(t):
    return t * jax.nn.sigmoid(t)


def _mod_kernel(c_ref, w_ref, b_ref, o_ref):
    s = _silu(c_ref[...])
    o_ref[0] = jnp.dot(s, w_ref[0], preferred_element_type=F32, precision=HIGHEST) + b_ref[0]


def _modulation(cond, w_mod, b_mod):
    depth, d, n = w_mod.shape
    rows = cond.shape[0]
    tn = 1536
    assert n % tn == 0
    return pl.pallas_call(
        _mod_kernel,
        grid=(depth, n // tn),
        in_specs=[pl.BlockSpec((rows, d), lambda i, j: (0, 0)),
                  pl.BlockSpec((1, d, tn), lambda i, j: (i, 0, j)),
                  pl.BlockSpec((1, 1, tn), lambda i, j: (i, 0, j))],
        out_specs=pl.BlockSpec((1, rows, tn), lambda i, j: (i, 0, j)),
        out_shape=jax.ShapeDtypeStruct((depth, rows, n), F32),
        compiler_params=_cparams(2),
        name="modulation",
    )(cond, w_mod, b_mod.reshape(depth, 1, n))


def _swap_halves(t):
    w = t.shape[-1]
    lane = lax.broadcasted_iota(jnp.int32, t.shape, 1)
    from_hi = pltpu.roll(t, w - AXIS_ROT // 2, axis=1)
    from_lo = pltpu.roll(t, AXIS_ROT // 2, axis=1)
    return jnp.where(lane % AXIS_ROT < AXIS_ROT // 2, from_hi, from_lo)


def _inproj_kernel(rope, x_ref, m_ref, g_ref, wqk_ref, wt_ref, wcv_ref, cs_ref, sn_ref,
                   q_ref, k_ref, vt_ref, hy_ref, cv_ref):
    x = x_ref[0]
    h = _rms(x, g_ref[...]) * (1.0 + m_ref[0, 1:2, :]) + m_ref[0, 0:1, :]
    hb = h.astype(BF16)
    p = jnp.dot(hb, wqk_ref[...], preferred_element_type=F32)
    q = p[:, :ATTN_W]
    k = p[:, ATTN_W:]
    if rope:
        cs = cs_ref[...]
        sn = sn_ref[...]
        cq = jnp.concatenate([cs] * (ATTN_W // LANES), axis=1)
        sq = jnp.concatenate([sn] * (ATTN_W // LANES), axis=1)
        q = q * cq + _swap_halves(q) * sq
        k = k * cs + _swap_halves(k) * sn
    q_ref[0] = (q * (HEAD_DIM ** -0.5 * LOG2E)).astype(BF16)
    k_ref[0] = jnp.concatenate([k, pltpu.roll(k, HEAD_DIM, axis=1)], axis=1).astype(BF16)
    u = jnp.dot(hb, wcv_ref[...], preferred_element_type=F32)
    ch = u.shape[1] // 2
    cv_ref[0] = (u[:, :ch] * jax.nn.sigmoid(u[:, ch:])).astype(BF16)
    pt = lax.dot_general(wt_ref[...], hb, NT, preferred_element_type=F32)
    vt_ref[0] = jnp.concatenate([pt[:KV_W], pt[HEAD_DIM:KV_W], pt[:HEAD_DIM]], axis=0).astype(BF16)
    hy_ref[0] = pt[KV_W:].astype(BF16)


def _inproj(x, mods, mod_row, g_pre, wqkv, whyt, wcv, layer, cs, sn, tm):
    b, l, d = x.shape
    rope = cs is not None
    if not rope:
        cs = jnp.zeros((l, LANES), F32)
        sn = cs
    nhy = whyt.shape[1] - KV_W
    ncv = wcv.shape[2] // 2
    mrow = (lambda bi: bi) if mod_row is None else (lambda bi: mod_row)
    return pl.pallas_call(
        functools.partial(_inproj_kernel, rope),
        grid=(b, l // tm),
        in_specs=[pl.BlockSpec((1, tm, d), lambda bi, j: (bi, j, 0)),
                  pl.BlockSpec((1, 6, d), lambda bi, j: (mrow(bi), 0, 0)),
                  _layer_spec((1, d), layer),
                  _layer_spec(wqkv.shape[1:], layer),
                  _layer_spec(whyt.shape[1:], layer),
                  _layer_spec(wcv.shape[1:], layer),
                  pl.BlockSpec((tm, LANES), lambda bi, j: (j, 0)),
                  pl.BlockSpec((tm, LANES), lambda bi, j: (j, 0))],
        out_specs=[pl.BlockSpec((1, tm, ATTN_W), lambda bi, j: (bi, j, 0)),
                   pl.BlockSpec((1, tm, 2 * KV_W), lambda bi, j: (bi, j, 0)),
                   pl.BlockSpec((1, 2 * KV_W, tm), lambda bi, j: (bi, 0, j)),
                   pl.BlockSpec((1, nhy, tm), lambda bi, j: (bi, 0, j)),
                   pl.BlockSpec((1, tm, ncv), lambda bi, j: (bi, j, 0))],
        out_shape=[jax.ShapeDtypeStruct((b, l, ATTN_W), BF16),
                   jax.ShapeDtypeStruct((b, l, 2 * KV_W), BF16),
                   jax.ShapeDtypeStruct((b, 2 * KV_W, l), BF16),
                   jax.ShapeDtypeStruct((b, nhy, l), BF16),
                   jax.ShapeDtypeStruct((b, l, ncv), BF16)],
        compiler_params=_cparams(2),
        name="inproj",
    )(x, mods, g_pre, wqkv, whyt, wcv, cs, sn)


def _half_variants(a, axis):
    idx = lax.broadcasted_iota(jnp.int32, (LANES, 1) if axis == 0 else (1, LANES), axis)
    lo = idx < HEAD_DIM
    x, y = (a[:LANES], a[LANES:]) if axis == 0 else (a[:, :LANES], a[:, LANES:])
    zero = jnp.zeros_like(x)
    return {(0, 0): jnp.where(lo, x, zero), (0, 1): jnp.where(lo, zero, y),
            (1, 0): jnp.where(lo, y, zero), (1, 1): jnp.where(lo, zero, x)}


def _attend(jobs, q, sink_ref, o_ref):
    row_lo = lax.broadcasted_iota(jnp.int32, (LANES, 1), 0) < HEAD_DIM
    units = [(job, kv) for job in jobs for kv in range(N_KV_HEADS)]

    def scores(unit):
        (row0, rows, kvars, _, _), kv = unit
        qs = jnp.concatenate([q[row0:row0 + rows, s * LANES:(s + 1) * LANES]
                              for s in (2 * kv, 2 * kv + 1)], axis=0)
        kcat = jnp.concatenate([kd[(kv, half)] for half in range(2) for kd in kvars], axis=0)
        return lax.dot_general(kcat, qs, NT, preferred_element_type=F32)

    def finish(unit, s_all):
        (row0, rows, kvars, vtvars, masks), kv = unit
        slabs = (2 * kv, 2 * kv + 1)
        vtcat = jnp.concatenate([vd[(kv, half)] for half in range(2) for vd in vtvars], axis=1)
        probs, denoms = [], []
        off = 0
        for half in range(2):
            sink = jnp.concatenate([jnp.full((1, rows), sink_ref[0, 2 * s + half] * LOG2E, F32)
                                    for s in slabs], axis=1)
            m = sink
            segs = []
            for kd, mk in zip(kvars, masks):
                n = kd[(kv, half)].shape[0]
                s = s_all[off:off + n]
                off += n
                if mk is not None:
                    s = s + mk
                segs.append(s)
                m = jnp.maximum(m, jnp.max(s, axis=0, keepdims=True))
            denom = jnp.exp2(sink - m)
            for s in segs:
                p = jnp.exp2(s - m)
                denom = denom + jnp.sum(p, axis=0, keepdims=True)
                probs.append(p.astype(BF16))
            denoms.append(denom)
        ot = jnp.dot(vtcat, jnp.concatenate(probs, axis=0), preferred_element_type=F32)
        ot = ot / jnp.where(row_lo, denoms[0], denoms[1])
        for t, s in enumerate(slabs):
            o_ref[0, row0:row0 + rows, s * LANES:(s + 1) * LANES] = (
                ot[:, t * rows:(t + 1) * rows].T.astype(o_ref.dtype))

    pending = {}
    for i in range(len(units) + ATTN_LOOKAHEAD):
        if i < len(units):
            pending[i] = scores(units[i])
        if i >= ATTN_LOOKAHEAD:
            finish(units[i - ATTN_LOOKAHEAD], pending.pop(i - ATTN_LOOKAHEAD))


def _win_attn_kernel(sink_ref, q_ref, kp_ref, kc_ref, kn_ref, vp_ref, vc_ref, vn_ref,
                     kx_ref, vx_ref, bf_ref, bm_ref, bl_ref, o_ref):
    tq = q_ref.shape[1]
    n_sub = tq // WINDOW
    span = 3 * WINDOW
    q = q_ref[0]
    kloc = _half_variants(jnp.concatenate([kp_ref[0], kc_ref[0], kn_ref[0]], axis=0), 1)
    vloc = _half_variants(jnp.concatenate([vp_ref[0], vc_ref[0], vn_ref[0]], axis=1), 0)
    kctx = _half_variants(kx_ref[0], 1)
    vctx = _half_variants(vx_ref[0], 0)
    jobs = []
    for sb in range(n_sub):
        bias = (bf_ref if sb == 0 else bl_ref if sb == n_sub - 1 else bm_ref)[0]
        ks = {key: val[sb * WINDOW:sb * WINDOW + span] for key, val in kloc.items()}
        vs = {key: val[:, sb * WINDOW:sb * WINDOW + span] for key, val in vloc.items()}
        jobs.append((sb * WINDOW, WINDOW, [ks, kctx], [vs, vctx], [bias, None]))
    _attend(jobs, q, sink_ref, o_ref)


def _band_bias():
    span = 3 * WINDOW
    c = np.arange(span)[:, None]
    r = np.arange(2 * WINDOW)[None, :] % WINDOW
    band = (c >= r) & (c <= r + 2 * WINDOW)
    variants = [band & (c >= WINDOW), band, band & (c < 2 * WINDOW)]
    return jnp.asarray(np.where(np.stack(variants), 0.0, NEG_INF), F32)


def _window_attention(q, k, v, k_ctx, v_ctx, sink, tq):
    b, l, _ = q.shape
    n_ctx = k_ctx.shape[1]
    assert tq % WINDOW == 0 and l % tq == 0 and tq >= 2 * WINDOW
    r = tq // WINDOW
    nwb = l // WINDOW
    nj = l // tq
    kvw = k.shape[2]
    i_prev = lambda j: jnp.maximum(j * r - 1, 0)
    i_next = lambda j: jnp.minimum((j + 1) * r, nwb - 1)
    prev = pl.BlockSpec((1, WINDOW, kvw), lambda bi, j: (bi, i_prev(j), 0))
    cur = pl.BlockSpec((1, tq, kvw), lambda bi, j: (bi, j, 0))
    nxt = pl.BlockSpec((1, WINDOW, kvw), lambda bi, j: (bi, i_next(j), 0))
    ctx = pl.BlockSpec((1, n_ctx, kvw), lambda bi, j: (bi, 0, 0))
    prev_t = pl.BlockSpec((1, kvw, WINDOW), lambda bi, j: (bi, 0, i_prev(j)))
    cur_t = pl.BlockSpec((1, kvw, tq), lambda bi, j: (bi, 0, j))
    nxt_t = pl.BlockSpec((1, kvw, WINDOW), lambda bi, j: (bi, 0, i_next(j)))
    ctx_t = pl.BlockSpec((1, kvw, n_ctx), lambda bi, j: (bi, 0, 0))
    bshape = (1, 3 * WINDOW, 2 * WINDOW)
    bias_first = pl.BlockSpec(bshape, lambda bi, j: (jnp.where(j == 0, 0, 1), 0, 0))
    bias_mid = pl.BlockSpec(bshape, lambda bi, j: (1, 0, 0))
    bias_last = pl.BlockSpec(bshape, lambda bi, j: (jnp.where(j == nj - 1, 2, 1), 0, 0))
    bias = _band_bias()
    return pl.pallas_call(
        _win_attn_kernel,
        grid=(b, nj),
        in_specs=[pl.BlockSpec(memory_space=pltpu.SMEM),
                  pl.BlockSpec((1, tq, ATTN_W), lambda bi, j: (bi, j, 0)),
                  prev, cur, nxt, prev_t, cur_t, nxt_t, ctx, ctx_t, bias_first, bias_mid, bias_last],
        out_specs=pl.BlockSpec((1, tq, ATTN_W), lambda bi, j: (bi, j, 0)),
        out_shape=jax.ShapeDtypeStruct((b, l, ATTN_W), BF16),
        compiler_params=_cparams(2),
        name="window_attention",
    )(sink, q, k, k, k, v, v, v, k_ctx, v_ctx, bias, bias, bias)


def _ctx_attn_kernel(sink_ref, q_ref, k_ref, v_ref, o_ref):
    job = (0, q_ref.shape[1], [_half_variants(k_ref[0], 1)], [_half_variants(v_ref[0], 0)], [None])
    _attend([job], q_ref[0], sink_ref, o_ref)


def _context_attention(q, k, v, sink):
    b, l, _ = q.shape
    kvw = k.shape[2]
    return pl.pallas_call(
        _ctx_attn_kernel,
        grid=(b,),
        in_specs=[pl.BlockSpec(memory_space=pltpu.SMEM),
                  pl.BlockSpec((1, l, ATTN_W), lambda bi: (bi, 0, 0)),
                  pl.BlockSpec((1, l, kvw), lambda bi: (bi, 0, 0)),
                  pl.BlockSpec((1, kvw, l), lambda bi: (bi, 0, 0))],
        out_specs=pl.BlockSpec((1, l, ATTN_W), lambda bi: (bi, 0, 0)),
        out_shape=jax.ShapeDtypeStruct((b, l, ATTN_W), BF16),
        compiler_params=_cparams(1),
        name="context_attention",
    )(sink, q, k, v)


def _taps_kernel(seq_len, z_ref, w1_ref, b1_ref, f1_ref, w2_ref, b2_ref, f2_ref, w3t_ref,
                 dl_ref, o_ref):
    tn = z_ref.shape[1]
    ch = dl_ref.shape[0]
    z = z_ref[...]
    h = jnp.sin(f1_ref[...] * (jnp.dot(w1_ref[...], z, preferred_element_type=F32,
                                       precision=HIGHEST) + b1_ref[...]))
    h = jnp.sin(f2_ref[...] * (jnp.dot(w2_ref[...], h, preferred_element_type=F32,
                                       precision=HIGHEST) + b2_ref[...]))
    full = jnp.dot(w3t_ref[...], h.astype(BF16), preferred_element_type=F32)
    n0 = pl.program_id(0) * tn
    is_fwd = n0 < seq_len
    decay = jnp.exp(-z[0:1, :] * dl_ref[...])
    pos = n0 + lax.broadcasted_iota(jnp.int32, (1, tn), 1)
    for o in range(HY_ORDER):
        fwd = full[o * 2 * ch:o * 2 * ch + ch]
        bwd = full[o * 2 * ch + ch:(o + 1) * 2 * ch]
        taps = jnp.where(is_fwd, fwd, bwd) * decay
        o_ref[o] = jnp.where(pos == seq_len, 0.0, taps).astype(o_ref.dtype)


def _filter_tables(l):
    t = jnp.linspace(0.0, 1.0, l, dtype=F32)[:, None]
    w = (2.0 * math.pi / l) * jnp.arange(l, dtype=F32)[:, None]
    f = jnp.linspace(1e-4, HY_BANDS - 1, HY_BANDS, dtype=F32)[None, :]
    z = jnp.concatenate([t, jnp.cos(f * w), -jnp.sin(f * w)], axis=-1)
    zc = jnp.concatenate([z, z[-1:], jnp.flip(z[1:], axis=0)], axis=0)
    return jnp.pad(zc, ((0, 0), (0, LANES - HY_EMB))).T


def _hyena_taps(l, zct, w1t, b1, f1, w2t, b2, f2, w3t, deltas, layer):
    ch = deltas.shape[0]
    fw = w2t.shape[1]
    tn = min(1024, l)
    return pl.pallas_call(
        functools.partial(_taps_kernel, l),
        grid=(2 * l // tn,),
        in_specs=[pl.BlockSpec((LANES, tn), lambda j: (0, j)),
                  _layer_spec((fw, LANES), layer),
                  _layer_spec((fw, 1), layer),
                  _layer_spec((fw, 1), layer),
                  _layer_spec((fw, fw), layer),
                  _layer_spec((fw, 1), layer),
                  _layer_spec((fw, 1), layer),
                  _layer_spec(w3t.shape[1:], layer),
                  _full_spec((ch, 1))],
        out_specs=pl.BlockSpec((HY_ORDER, ch, tn), lambda j: (0, 0, j)),
        out_shape=jax.ShapeDtypeStruct((HY_ORDER, ch, 2 * l), BF16),
        compiler_params=_cparams(1),
        name="hyena_taps",
    )(zct, w1t, b1, f1, w2t, b2, f2, w3t, deltas)


def _dft_consts(na, nb):
    a = np.arange(na)
    fa = np.exp(-2j * np.pi * np.outer(a, a) / na)
    b = np.arange(nb)
    g = np.exp(-2j * np.pi * np.outer(b, b) / nb)
    fr, fi = fa.real, fa.imag
    gr, gi = g.real, g.imag
    h = na // 2
    m1 = np.block([[fr[:, :h], -fi[:, :h]], [fi[:, :h], fr[:, :h]]])
    m1f = np.concatenate([fr, fi], axis=0)
    m2 = np.block([[gr, gi], [-gi, gr]])
    m2i = np.block([[gr, -gi], [gi, gr]])
    m1i = np.block([[fr[:h], fi[:h]], [-fi[:h], fr[:h]]])
    tw = np.exp(-2j * np.pi * np.outer(a, b) / (na * nb))
    bf = lambda m: jnp.asarray(m, dtype=F32).astype(BF16)
    return dict(m1=bf(m1), m1f=bf(m1f), m2=bf(m2), m2i=bf(m2i), m1i=bf(m1i),
                tr=bf(tw.real), ti=bf(tw.imag))


def _stage2(a_all, tr, ti, m2):
    na, nb = tr.shape
    n = a_all.shape[1] // nb
    lhs = []
    for i in range(n):
        ar = a_all[:na, i * nb:(i + 1) * nb]
        ai = a_all[na:, i * nb:(i + 1) * nb]
        lhs.append(jnp.concatenate([ar * tr - ai * ti, ar * ti + ai * tr], axis=1))
    return jnp.dot(jnp.concatenate(lhs, axis=0), m2, preferred_element_type=F32)


def _filter_spectra(taps, m1f, m2, tr, ti):
    na, nb = tr.shape
    a_all = jnp.dot(m1f, jnp.concatenate(taps, axis=1), preferred_element_type=F32).astype(BF16)
    x = (_stage2(a_all, tr, ti, m2) * (1.0 / (na * nb))).astype(BF16)
    return [x[i * na:(i + 1) * na] for i in range(len(taps))]


def _shift_prev(x):
    rows, nb = x.shape
    lane = lax.broadcasted_iota(jnp.int32, x.shape, 1)
    row = lax.broadcasted_iota(jnp.int32, x.shape, 0)
    r = pltpu.roll(x, 1, axis=1)
    r2 = pltpu.roll(r, 1, axis=0)
    return jnp.where(lane == 0, jnp.where(row == 0, 0.0, r2), r)


def _shift_next(x):
    rows, nb = x.shape
    lane = lax.broadcasted_iota(jnp.int32, x.shape, 1)
    row = lax.broadcasted_iota(jnp.int32, x.shape, 0)
    r = pltpu.roll(x, nb - 1, axis=1)
    r2 = pltpu.roll(r, rows - 1, axis=0)
    return jnp.where(lane == nb - 1, jnp.where(row == rows - 1, 0.0, r2), r)


def _fft_conv(zs, hfs, m1, m2, m2i, m1i, tr, ti):
    na, nb = tr.shape
    n = len(zs)
    h = na // 2
    zc = jnp.concatenate([jnp.concatenate([zr, zi], axis=0) for zr, zi in zs], axis=1)
    a_all = jnp.dot(m1, zc.astype(BF16), preferred_element_type=F32).astype(BF16)
    x_all = _stage2(a_all, tr, ti, m2).astype(BF16)
    ys = []
    for i in range(n):
        x = x_all[i * na:(i + 1) * na]
        xr, xi = x[:, :nb], x[:, nb:]
        hr, hi = hfs[i][:, :nb], hfs[i][:, nb:]
        ys.append(jnp.concatenate([xr * hr - xi * hi, xr * hi + xi * hr], axis=1))
    b_all = jnp.dot(jnp.concatenate(ys, axis=0), m2i, preferred_element_type=F32).astype(BF16)
    rhs = []
    for i in range(n):
        bm = b_all[i * na:(i + 1) * na]
        br, bi = bm[:, :nb], bm[:, nb:]
        rhs.append(jnp.concatenate([br * tr + bi * ti, bi * tr - br * ti], axis=0))
    yo = jnp.dot(m1i, jnp.concatenate(rhs, axis=1), preferred_element_type=F32)
    return [(yo[:h, i * nb:(i + 1) * nb], yo[h:, i * nb:(i + 1) * nb]) for i in range(n)]


def _hyena_kernel(sw_ref, sb_ref, hb_ref, x1_ref, x2_ref, z_ref, taps_ref,
                  m1_ref, m1f_ref, m2_ref, m2i_ref, m1i_ref, tr_ref, ti_ref, o_ref):
    nbatch, ct = x1_ref.shape[:2]
    nch = hb_ref.shape[1]
    tr, ti = tr_ref[...], ti_ref[...]
    mats = (m1_ref[...], m2_ref[...], m2i_ref[...], m1i_ref[...], tr, ti)
    gates = (x1_ref, x2_ref)
    c0 = pl.program_id(0) * ct

    def short(ref, stream, b, c):
        ch = stream * nch + c0 + c
        x = ref[b, c].astype(F32)
        return (sw_ref[0, ch] * _shift_prev(x) + sw_ref[1, ch] * x
                + sw_ref[2, ch] * _shift_next(x) + sb_ref[0, ch])

    specs = _filter_spectra([taps_ref[o, c] for o in range(HY_ORDER) for c in range(ct)],
                            m1f_ref[...], m2_ref[...], tr, ti)
    seqs = [(c, p) for c in range(ct) for p in range(nbatch // 2)]
    zz = [(short(z_ref, 2, 2 * p, c), short(z_ref, 2, 2 * p + 1, c)) for c, p in seqs]
    for o in range(HY_ORDER):
        ys = _fft_conv(zz, [specs[o * ct + c] for c, _ in seqs], *mats)
        zz = [tuple(short(gates[o], o, 2 * p + q, c) * (ys[i][q] + hb_ref[o, c0 + c] * zz[i][q])
                    for q in range(2)) for i, (c, p) in enumerate(seqs)]
    for i, (c, p) in enumerate(seqs):
        for q in range(2):
            o_ref[2 * p + q, c] = zz[i][q].astype(o_ref.dtype)


def _hyena_long(hyt, taps, short_w, short_b, hy_bias, dc, ct):
    b, c3, l = hyt.shape
    ch = c3 // 3
    nb = LANES
    na = 2 * l // nb
    assert b % 2 == 0 and ct % 2 == 0 and ch % ct == 0
    h4 = hyt.reshape(b, c3, na // 2, nb)
    t4 = taps.reshape(HY_ORDER, ch, na, nb)
    blk = (b, ct, na // 2, nb)
    nct = ch // ct
    smem = pl.BlockSpec(memory_space=pltpu.SMEM)
    out = pl.pallas_call(
        _hyena_kernel,
        grid=(nct,),
        in_specs=[smem, smem, smem,
                  pl.BlockSpec(blk, lambda j: (0, j, 0, 0)),
                  pl.BlockSpec(blk, lambda j: (0, nct + j, 0, 0)),
                  pl.BlockSpec(blk, lambda j: (0, 2 * nct + j, 0, 0)),
                  pl.BlockSpec((HY_ORDER, ct, na, nb), lambda j: (0, j, 0, 0)),
                  _full_spec((2 * na, na)), _full_spec((2 * na, na)),
                  _full_spec((2 * nb, 2 * nb)), _full_spec((2 * nb, 2 * nb)),
                  _full_spec((na, 2 * na)), _full_spec((na, nb)), _full_spec((na, nb))],
        out_specs=pl.BlockSpec(blk, lambda j: (0, j, 0, 0)),
        out_shape=jax.ShapeDtypeStruct((b, ch, na // 2, nb), BF16),
        compiler_params=_cparams(1),
        name="hyena_long",
    )(short_w, short_b, hy_bias, h4, h4, h4, t4,
      dc["m1"], dc["m1f"], dc["m2"], dc["m2i"], dc["m1i"], dc["tr"], dc["ti"])
    return out.reshape(b, ch, l)


def _dense_dft_consts(l):
    n = 2 * l
    k = np.arange(n)
    ang = 2 * np.pi * np.outer(k, k) / n
    cos, sin = np.cos(ang), np.sin(ang)
    fwd = np.concatenate([cos, -sin], axis=1)
    inv = np.concatenate([cos[:, :l], -sin[:, :l]], axis=0) / n
    bf = lambda m: jnp.asarray(m, dtype=F32).astype(BF16)
    return dict(fwd_half=bf(fwd[:l]), fwd=bf(fwd), inv=bf(inv))


def _hyena_short_kernel(sw_ref, sb_ref, hb_ref, x1_ref, x2_ref, z_ref, taps_ref,
                        fh_ref, ff_ref, inv_ref, o_ref):
    nbatch, ct, l = x1_ref.shape
    n = 2 * l
    lane = lax.broadcasted_iota(jnp.int32, (ct, l), 1)

    def short(ref, stream, b):
        x = ref[b].astype(F32)
        prev = jnp.where(lane == 0, 0.0, pltpu.roll(x, 1, axis=1))
        nxt = jnp.where(lane == l - 1, 0.0, pltpu.roll(x, l - 1, axis=1))
        return sw_ref[stream, 0] * prev + sw_ref[stream, 1] * x + sw_ref[stream, 2] * nxt + sb_ref[stream]

    gates = (x1_ref, x2_ref)
    zz = [short(z_ref, 2, b) for b in range(nbatch)]
    for o in range(HY_ORDER):
        hf = jnp.dot(taps_ref[o].astype(BF16), ff_ref[...], preferred_element_type=F32)
        hr, hi = hf[:, :n], hf[:, n:]
        x_all = jnp.dot(jnp.concatenate(zz, axis=0).astype(BF16), fh_ref[...],
                        preferred_element_type=F32)
        ys = []
        for b in range(nbatch):
            x = x_all[b * ct:(b + 1) * ct]
            xr, xi = x[:, :n], x[:, n:]
            ys.append(jnp.concatenate([xr * hr - xi * hi, xr * hi + xi * hr], axis=1).astype(BF16))
        conv = jnp.dot(jnp.concatenate(ys, axis=0), inv_ref[...], preferred_element_type=F32)
        zz = [short(gates[o], o, b) * (conv[b * ct:(b + 1) * ct] + hb_ref[o] * zz[b])
              for b in range(nbatch)]
    for b in range(nbatch):
        o_ref[b] = zz[b].astype(o_ref.dtype)


def _hyena_short(hyt, taps, short_w3, short_b3, hy_bias3, dd, ct):
    b, c3, l = hyt.shape
    ch = c3 // 3
    n = 2 * l
    nct = ch // ct
    blk = (b, ct, l)
    return pl.pallas_call(
        _hyena_short_kernel,
        grid=(nct,),
        in_specs=[pl.BlockSpec((3, 3, ct, 1), lambda j: (0, 0, j, 0)),
                  pl.BlockSpec((3, ct, 1), lambda j: (0, j, 0)),
                  pl.BlockSpec((HY_ORDER, ct, 1), lambda j: (0, j, 0)),
                  pl.BlockSpec(blk, lambda j: (0, j, 0)),
                  pl.BlockSpec(blk, lambda j: (0, nct + j, 0)),
                  pl.BlockSpec(blk, lambda j: (0, 2 * nct + j, 0)),
                  pl.BlockSpec((HY_ORDER, ct, n), lambda j: (0, j, 0)),
                  _full_spec((l, 2 * n)), _full_spec((n, 2 * n)), _full_spec((2 * n, l))],
        out_specs=pl.BlockSpec(blk, lambda j: (0, j, 0)),
        out_shape=jax.ShapeDtypeStruct((b, ch, l), BF16),
        compiler_params=_cparams(1),
        name="hyena_short",
    )(short_w3, short_b3, hy_bias3, hyt, hyt, hyt, taps, dd["fwd_half"], dd["fwd"], dd["inv"])


def _mixout_kernel(a_ref, hy_ref, up_ref, uc_ref, un_ref, x_ref, m_ref, g_ref, wa_ref, wh_ref, wc_ref,
                   dw_ref, db_ref, lg_ref, lb_ref, o_ref, buf, cv_buf):
    tm = uc_ref.shape[1]
    ch = cv_buf.shape[1]
    j = pl.program_id(1)
    nj = pl.num_programs(1)
    y = jnp.dot(a_ref[0], wa_ref[...], preferred_element_type=F32)
    y = y + pl.dot(hy_ref[0], wh_ref[...], trans_a=True)

    def glu(u):
        return u.astype(F32)

    buf[0, 0:HALO] = jnp.where(j > 0, glu(up_ref[0]), 0.0)
    buf[0, HALO:HALO + tm] = glu(uc_ref[0])
    buf[0, HALO + tm:] = jnp.where(j < nj - 1, glu(un_ref[0]), 0.0)
    span = tm + 2 * HALO - SUBLANES
    for s in range(1, SUBLANES):
        buf[s, 0:span] = buf[0, pl.ds(s, span), :]
    pad = (CV_K - 1) // 2
    rc = 128
    for r0 in range(0, tm, rc):
        acc = jnp.zeros((rc, ch), F32) + db_ref[...]
        for k in range(CV_K):
            off = HALO - pad + k
            acc = acc + dw_ref[k:k + 1, :] * buf[off % SUBLANES, pl.ds(r0 + off - off % SUBLANES, rc), :]
        mu = jnp.mean(acc, axis=-1, keepdims=True)
        cen = acc - mu
        var = jnp.mean(cen * cen, axis=-1, keepdims=True)
        cv = cen * lax.rsqrt(var + EPS) * lg_ref[...] + lb_ref[...]
        cv_buf[r0:r0 + rc] = _silu(cv).astype(cv_buf.dtype)
    y = y + jnp.dot(cv_buf[...], wc_ref[...], preferred_element_type=F32)
    o_ref[0] = x_ref[0] + m_ref[0, 2:3, :] * _rms(y, g_ref[...])


def _mixout(attn, hyt, u, x, mods, mod_row, g_post, wa, wh, wc, dw_w, dw_b, ln_g, ln_b, layer, tm):
    b, l, d = x.shape
    c2 = u.shape[2]
    ch = c2
    r = tm // HALO
    nh = l // HALO
    mrow = (lambda bi: bi) if mod_row is None else (lambda bi: mod_row)
    return pl.pallas_call(
        _mixout_kernel,
        grid=(b, l // tm),
        in_specs=[pl.BlockSpec((1, tm, attn.shape[2]), lambda bi, j: (bi, j, 0)),
                  pl.BlockSpec((1, hyt.shape[1], tm), lambda bi, j: (bi, 0, j)),
                  pl.BlockSpec((1, HALO, c2), lambda bi, j: (bi, jnp.maximum(j * r - 1, 0), 0)),
                  pl.BlockSpec((1, tm, c2), lambda bi, j: (bi, j, 0)),
                  pl.BlockSpec((1, HALO, c2), lambda bi, j: (bi, jnp.minimum((j + 1) * r, nh - 1), 0)),
                  pl.BlockSpec((1, tm, d), lambda bi, j: (bi, j, 0)),
                  pl.BlockSpec((1, 6, d), lambda bi, j: (mrow(bi), 0, 0)),
                  _layer_spec((1, d), layer),
                  _layer_spec(wa.shape[1:], layer),
                  _layer_spec(wh.shape[1:], layer),
                  _layer_spec(wc.shape[1:], layer),
                  _layer_spec((CV_K, ch), layer),
                  _layer_spec((1, ch), layer), _layer_spec((1, ch), layer), _layer_spec((1, ch), layer)],
        out_specs=pl.BlockSpec((1, tm, d), lambda bi, j: (bi, j, 0)),
        out_shape=jax.ShapeDtypeStruct((b, l, d), F32),
        scratch_shapes=[pltpu.VMEM((SUBLANES, tm + 2 * HALO, ch), F32), pltpu.VMEM((tm, ch), BF16)],
        compiler_params=_cparams(2),
        name="mix_outproj",
    )(attn, hyt, u, u, u, x, mods, g_post, wa, wh, wc, dw_w, dw_b, ln_g, ln_b)


def _ffn_kernel(n_chunks, xp_ref, xc_ref, xn_ref, m_ref, gpre_ref, gpost_ref, wu_ref, wg_ref,
                dw_ref, db_ref, wd_ref, o_ref):
    tm = xc_ref.shape[1]
    dff = wu_ref.shape[1]
    cw = dff // n_chunks
    j = pl.program_id(1)
    nj = pl.num_programs(1)
    shift, scale, gate = m_ref[0, 3:4, :], m_ref[0, 4:5, :], m_ref[0, 5:6, :]
    gpre = gpre_ref[...]

    def prep(x):
        return (_rms(x, gpre) * (1.0 + scale) + shift).astype(BF16)

    xc = xc_ref[0]
    hb = prep(xc)
    hext = jnp.concatenate([prep(xp_ref[0]), hb, prep(xn_ref[0])], axis=0)
    y = jnp.zeros((tm, o_ref.shape[2]), F32)
    for c in range(n_chunks):
        cols = slice(c * cw, (c + 1) * cw)
        g = jnp.dot(hext, wg_ref[:, cols], preferred_element_type=F32)
        u = jnp.dot(hb, wu_ref[:, cols], preferred_element_type=F32)
        g = jnp.concatenate([jnp.where(j > 0, g[:HALO], 0.0), g[HALO:HALO + tm],
                             jnp.where(j < nj - 1, g[HALO + tm:], 0.0)], axis=0)
        rows = tm + 2 * HALO
        g_prev = pltpu.roll(g, 1, axis=0)[HALO:HALO + tm]
        g_next = pltpu.roll(g, rows - 1, axis=0)[HALO:HALO + tm]
        conv = (dw_ref[0:1, cols] * g_prev + dw_ref[1:2, cols] * g[HALO:HALO + tm]
                + dw_ref[2:3, cols] * g_next + db_ref[:, cols])
        act = (_silu(conv) * u).astype(BF16)
        y = y + jnp.dot(act, wd_ref[cols, :], preferred_element_type=F32)
    o_ref[0] = xc + gate * _rms(y, gpost_ref[...])


def _ffn(x, mods, mod_row, g_pre, g_post, wu, wg, dw_w, dw_b, wd, layer, tm, n_chunks):
    b, l, d = x.shape
    dff = wu.shape[2]
    r = tm // HALO
    nh = l // HALO
    mrow = (lambda bi: bi) if mod_row is None else (lambda bi: mod_row)
    return pl.pallas_call(
        functools.partial(_ffn_kernel, n_chunks),
        grid=(b, l // tm),
        in_specs=[pl.BlockSpec((1, HALO, d), lambda bi, j: (bi, jnp.maximum(j * r - 1, 0), 0)),
                  pl.BlockSpec((1, tm, d), lambda bi, j: (bi, j, 0)),
                  pl.BlockSpec((1, HALO, d), lambda bi, j: (bi, jnp.minimum((j + 1) * r, nh - 1), 0)),
                  pl.BlockSpec((1, 6, d), lambda bi, j: (mrow(bi), 0, 0)),
                  _layer_spec((1, d), layer), _layer_spec((1, d), layer),
                  _layer_spec((d, dff), layer), _layer_spec((d, dff), layer),
                  _layer_spec((FFN_K, dff), layer), _layer_spec((1, dff), layer),
                  _layer_spec((dff, d), layer)],
        out_specs=pl.BlockSpec((1, tm, d), lambda bi, j: (bi, j, 0)),
        out_shape=jax.ShapeDtypeStruct((b, l, d), F32),
        compiler_params=_cparams(2),
        name="conv_ffn",
    )(x, x, x, mods, g_pre, g_post, wu, wg, dw_w, dw_b, wd)


def _rope_tables(l):
    rows = l // GRID_W
    row = jnp.repeat(jnp.arange(rows, dtype=F32), GRID_W)
    col = jnp.tile(jnp.arange(GRID_W, dtype=F32), rows)
    inv = ROPE_THETA ** (-jnp.arange(0, AXIS_ROT, 2, dtype=F32) / AXIS_ROT)
    ar, ac = row[:, None] * inv, col[:, None] * inv
    cos = jnp.concatenate([jnp.cos(ar), jnp.cos(ar), jnp.cos(ac), jnp.cos(ac)], axis=1)
    sin = jnp.concatenate([-jnp.sin(ar), jnp.sin(ar), -jnp.sin(ac), jnp.sin(ac)], axis=1)
    rep = LANES // HEAD_DIM
    return jnp.tile(cos, (1, rep)), jnp.tile(sin, (1, rep))


def kernel(x, c, ctx, c_ctx, w_mod, b_mod, g_pre_mix, g_post_mix, g_pre_ffn, g_post_ffn, w_in, attn_sink, hy_short_w, hy_short_b, hy_w1, hy_b1, hy_freq1, hy_w2, hy_b2, hy_freq2, hy_w3, hy_bias, cv_dw_w, cv_dw_b, cv_ln_g, cv_ln_b, w_out, w_up, ffn_dw_w, ffn_dw_b, w_down):
    b, l, d = x.shape
    lc = ctx.shape[1]
    depth = w_in.shape[0]
    hy_ch = hy_bias.shape[2]
    cv_ch = cv_dw_w.shape[2]
    dff = w_down.shape[1]
    k0 = ATTN_W
    hy0 = k0 + 2 * KV_W
    cv0 = hy0 + (HY_ORDER + 1) * hy_ch
    fw = hy_w2.shape[1]

    v0 = k0 + KV_W
    wqkv = w_in[:, :, :v0].astype(BF16)
    whyt = jnp.swapaxes(w_in[:, :, v0:cv0], 1, 2).astype(BF16)
    wcv = w_in[:, :, cv0:].astype(BF16)
    wo_a = w_out[:, :ATTN_W].astype(BF16)
    wo_h = w_out[:, ATTN_W:ATTN_W + hy_ch].astype(BF16)
    wo_c = w_out[:, ATTN_W + hy_ch:].astype(BF16)
    wu = w_up[:, :, :dff].astype(BF16)
    wg = w_up[:, :, dff:].astype(BF16)
    wd = w_down.astype(BF16)
    row = lambda a: a[:, None, :]
    gpm, gqm, gpf, gqf = row(g_pre_mix), row(g_post_mix), row(g_pre_ffn), row(g_post_ffn)
    col = lambda a: a[:, :, None]
    w1t = jnp.swapaxes(jnp.pad(hy_w1, ((0, 0), (0, LANES - HY_EMB), (0, 0))), 1, 2)
    w3t = jnp.swapaxes(hy_w3, 1, 2).astype(BF16)
    deltas = jnp.abs(jnp.linspace(math.log(HY_TARGET) / HY_SLOW, math.log(HY_TARGET) / HY_FAST,
                                  hy_ch, dtype=F32))[:, None]
    short_b2 = row(hy_short_b)
    sw5 = jnp.transpose(hy_short_w.reshape(depth, 3, 3, hy_ch), (0, 2, 1, 3))[..., None]
    sb4 = hy_short_b.reshape(depth, 3, hy_ch)[..., None]
    hb4 = hy_bias[..., None]

    n_rows = 8 * ((b + 1 + 7) // 8)
    cond = jnp.zeros((n_rows, d), F32).at[:b].set(c).at[b].set(c_ctx)
    mods = _modulation(cond, w_mod, b_mod).reshape(depth, n_rows, 6, d)

    cs, sn = _rope_tables(l)
    tab_l, tab_c = _filter_tables(l), _filter_tables(lc)
    dc = _dft_consts(2 * l // LANES, LANES)
    dd = _dense_dft_consts(lc)
    tm_in, tm_mix, tm_ffn, tq, hy_ct = min(1024, l), min(1024, l), min(512, l), min(2048, l), 16
    tmc = lc

    for i in range(depth):
        last = i == depth - 1
        m_i = mods[i]
        hyp = (w1t, col(hy_b1), col(hy_freq1), jnp.swapaxes(hy_w2, 1, 2), col(hy_b2), col(hy_freq2),
               w3t, deltas)

        q, k, v, hyt, cvu = _inproj(x, m_i, None, gpm, wqkv, whyt, wcv, i, cs, sn, tm_in)
        qc, kc, vc, hyt_c, cvu_c = _inproj(ctx, m_i, b, gpm, wqkv, whyt, wcv, i, None, None, tmc)

        attn = _window_attention(q, k, v, kc, vc, attn_sink[i][None, :], tq)
        taps = _hyena_taps(l, tab_l, *hyp, i)
        hy = _hyena_long(hyt, taps, hy_short_w[i], short_b2[i], hy_bias[i], dc, hy_ct)
        cvp = (cv_dw_w, row(cv_dw_b), row(cv_ln_g), row(cv_ln_b))
        x = _mixout(attn, hy, cvu, x, m_i, None, gqm, wo_a, wo_h, wo_c, *cvp, i, tm_mix)
        x = _ffn(x, m_i, None, gpf, gqf, wu, wg, ffn_dw_w, row(ffn_dw_b), wd, i, tm_ffn, 1)

        if not last:
            attn_c = _context_attention(qc, kc, vc, attn_sink[i][None, :])
            taps_c = _hyena_taps(lc, tab_c, *hyp, i)
            hy_c = _hyena_short(hyt_c, taps_c, sw5[i], sb4[i], hb4[i], dd, 32)
            ctx = _mixout(attn_c, hy_c, cvu_c, ctx, m_i, b, gqm, wo_a, wo_h, wo_c, *cvp, i, tmc)
            ctx = _ffn(ctx, m_i, b, gpf, gqf, wu, wg, ffn_dw_w, row(ffn_dw_b), wd, i, tmc, 1)
    return x
```

```python
import functools
import math

import numpy as np
import jax
import jax.numpy as jnp
from jax import lax
from jax.experimental import pallas as pl
from jax.experimental.pallas import tpu as pltpu

F32 = jnp.float32
BF16 = jnp.bfloat16
HIGHEST = lax.Precision.HIGHEST

N_HEADS = 8
N_KV_HEADS = 2
HEAD_DIM = 64
GROUP = N_HEADS // N_KV_HEADS
ATTN_W = N_HEADS * HEAD_DIM
KV_W = N_KV_HEADS * HEAD_DIM
WINDOW = 128
GRID_W = 64
ROPE_THETA = 10000.0
AXIS_ROT = HEAD_DIM // 2
HY_ORDER = 2
HY_EMB = 33
HY_BANDS = (HY_EMB - 1) // 2
HY_TARGET = 1e-2
HY_FAST = 0.3
HY_SLOW = 1.5
CV_K = 31
FFN_K = 3
EPS = 1e-6
NEG_INF = -1e30
LOG2E = math.log2(math.e)

LANES = 128
SUBLANES = 8
HALO = 16
ATTN_LOOKAHEAD = 2
VMEM_LIMIT = 56 * 1024 * 1024

NT = (((1,), (1,)), ((), ()))


def _cparams(n_axes):
    return pltpu.CompilerParams(dimension_semantics=("parallel",) * n_axes,
                                vmem_limit_bytes=VMEM_LIMIT)


def _full_spec(shape):
    nd = len(shape)
    return pl.BlockSpec(shape, lambda *_: (0,) * nd, pipeline_mode=pl.Buffered(1))


def _layer_spec(shape, layer):
    nd = len(shape)
    return pl.BlockSpec((None,) + tuple(shape), lambda *_: (layer,) + (0,) * nd,
                        pipeline_mode=pl.Buffered(1))


def _rms(t, g):
    return t * lax.rsqrt(jnp.mean(t * t, axis=-1, keepdims=True) + EPS) * g


def _silu(t):
    return t * jax.nn.sigmoid(t)


def _mod_kernel(c_ref, w_ref, b_ref, o_ref):
    s = _silu(c_ref[...])
    o_ref[0] = jnp.dot(s, w_ref[0], preferred_element_type=F32, precision=HIGHEST) + b_ref[0]


def _modulation(cond, w_mod, b_mod):
    depth, d, n = w_mod.shape
    rows = cond.shape[0]
    tn = 1536
    assert n % tn == 0
    return pl.pallas_call(
        _mod_kernel,
        grid=(depth, n // tn),
        in_specs=[pl.BlockSpec((rows, d), lambda i, j: (0, 0)),
                  pl.BlockSpec((1, d, tn), lambda i, j: (i, 0, j)),
                  pl.BlockSpec((1, 1, tn), lambda i, j: (i, 0, j))],
        out_specs=pl.BlockSpec((1, rows, tn), lambda i, j: (i, 0, j)),
        out_shape=jax.ShapeDtypeStruct((depth, rows, n), F32),
        compiler_params=_cparams(2),
        name="modulation",
    )(cond, w_mod, b_mod.reshape(depth, 1, n))


def _swap_halves(t):
    w = t.shape[-1]
    lane = lax.broadcasted_iota(jnp.int32, t.shape, 1)
    from_hi = pltpu.roll(t, w - AXIS_ROT // 2, axis=1)
    from_lo = pltpu.roll(t, AXIS_ROT // 2, axis=1)
    return jnp.where(lane % AXIS_ROT < AXIS_ROT // 2, from_hi, from_lo)


def _inproj_kernel(rope, x_ref, m_ref, g_ref, wqk_ref, wt_ref, wcv_ref, cs_ref, sn_ref,
                   q_ref, k_ref, vt_ref, hy_ref, cv_ref):
    x = x_ref[0]
    h = _rms(x, g_ref[...]) * (1.0 + m_ref[0, 1:2, :]) + m_ref[0, 0:1, :]
    hb = h.astype(BF16)
    q = jnp.dot(hb, wqk_ref[...], preferred_element_type=F32)
    if rope:
        cs = cs_ref[...]
        sn = sn_ref[...]
        cq = jnp.concatenate([cs] * (ATTN_W // LANES), axis=1)
        sq = jnp.concatenate([sn] * (ATTN_W // LANES), axis=1)
        q = q * cq + _swap_halves(q) * sq
    q_ref[0] = (q * (HEAD_DIM ** -0.5 * LOG2E)).astype(BF16)
    u = jnp.dot(hb, wcv_ref[...], preferred_element_type=F32)
    ch = u.shape[1] // 2
    cv_ref[0] = (u[:, :ch] * jax.nn.sigmoid(u[:, ch:])).astype(BF16)
    pt = lax.dot_general(wt_ref[...], hb, NT, preferred_element_type=F32)
    k = pt[:KV_W].T
    if rope:
        k = k * cs + _swap_halves(k) * sn
    k_ref[0] = jnp.concatenate([k, pltpu.roll(k, HEAD_DIM, axis=1)], axis=1).astype(BF16)
    vt = pt[KV_W:2 * KV_W]
    vt_ref[0] = jnp.concatenate([vt, vt[HEAD_DIM:], vt[:HEAD_DIM]], axis=0).astype(BF16)
    hy_ref[0] = pt[2 * KV_W:].astype(BF16)


def _inproj(x, mods, mod_row, g_pre, wqkv, whyt, wcv, layer, cs, sn, tm):
    b, l, d = x.shape
    rope = cs is not None
    if not rope:
        cs = jnp.zeros((l, LANES), F32)
        sn = cs
    nhy = whyt.shape[1] - 2 * KV_W
    ncv = wcv.shape[2] // 2
    mrow = (lambda bi: bi) if mod_row is None else (lambda bi: mod_row)
    return pl.pallas_call(
        functools.partial(_inproj_kernel, rope),
        grid=(b, l // tm),
        in_specs=[pl.BlockSpec((1, tm, d), lambda bi, j: (bi, j, 0)),
                  pl.BlockSpec((1, 6, d), lambda bi, j: (mrow(bi), 0, 0)),
                  _layer_spec((1, d), layer),
                  _layer_spec(wqkv.shape[1:], layer),
                  _layer_spec(whyt.shape[1:], layer),
                  _layer_spec(wcv.shape[1:], layer),
                  pl.BlockSpec((tm, LANES), lambda bi, j: (j, 0)),
                  pl.BlockSpec((tm, LANES), lambda bi, j: (j, 0))],
        out_specs=[pl.BlockSpec((1, tm, ATTN_W), lambda bi, j: (bi, j, 0)),
                   pl.BlockSpec((1, tm, 2 * KV_W), lambda bi, j: (bi, j, 0)),
                   pl.BlockSpec((1, 2 * KV_W, tm), lambda bi, j: (bi, 0, j)),
                   pl.BlockSpec((1, nhy, tm), lambda bi, j: (bi, 0, j)),
                   pl.BlockSpec((1, tm, ncv), lambda bi, j: (bi, j, 0))],
        out_shape=[jax.ShapeDtypeStruct((b, l, ATTN_W), BF16),
                   jax.ShapeDtypeStruct((b, l, 2 * KV_W), BF16),
                   jax.ShapeDtypeStruct((b, 2 * KV_W, l), BF16),
                   jax.ShapeDtypeStruct((b, nhy, l), BF16),
                   jax.ShapeDtypeStruct((b, l, ncv), BF16)],
        compiler_params=_cparams(2),
        name="inproj",
    )(x, mods, g_pre, wqkv, whyt, wcv, cs, sn)


def _half_variants(a, axis):
    idx = lax.broadcasted_iota(jnp.int32, (LANES, 1) if axis == 0 else (1, LANES), axis)
    lo = idx < HEAD_DIM
    x, y = (a[:LANES], a[LANES:]) if axis == 0 else (a[:, :LANES], a[:, LANES:])
    zero = jnp.zeros_like(x)
    return {(0, 0): jnp.where(lo, x, zero), (0, 1): jnp.where(lo, zero, y),
            (1, 0): jnp.where(lo, y, zero), (1, 1): jnp.where(lo, zero, x)}


def _attend(jobs, q, sink_ref, o_ref):
    row_lo = lax.broadcasted_iota(jnp.int32, (LANES, 1), 0) < HEAD_DIM
    units = [(job, kv) for job in jobs for kv in range(N_KV_HEADS)]

    def scores(unit):
        (row0, rows, kvars, _, _), kv = unit
        qs = jnp.concatenate([q[row0:row0 + rows, s * LANES:(s + 1) * LANES]
                              for s in (2 * kv, 2 * kv + 1)], axis=0)
        kcat = jnp.concatenate([kd[(kv, half)] for half in range(2) for kd in kvars], axis=0)
        return lax.dot_general(kcat, qs, NT, preferred_element_type=F32)

    def finish(unit, s_all):
        (row0, rows, kvars, vtvars, masks), kv = unit
        slabs = (2 * kv, 2 * kv + 1)
        vtcat = jnp.concatenate([vd[(kv, half)] for half in range(2) for vd in vtvars], axis=1)
        probs, denoms = [], []
        off = 0
        for half in range(2):
            sink = jnp.concatenate([jnp.full((1, rows), sink_ref[0, 2 * s + half] * LOG2E, F32)
                                    for s in slabs], axis=1)
            m = sink
            segs = []
            for kd, mk in zip(kvars, masks):
                n = kd[(kv, half)].shape[0]
                s = s_all[off:off + n]
                off += n
                if mk is not None:
                    s = s + mk
                segs.append(s)
                m = jnp.maximum(m, jnp.max(s, axis=0, keepdims=True))
            denom = jnp.exp2(sink - m)
            for s in segs:
                p = jnp.exp2(s - m)
                denom = denom + jnp.sum(p, axis=0, keepdims=True)
                probs.append(p.astype(BF16))
            denoms.append(denom)
        ot = jnp.dot(vtcat, jnp.concatenate(probs, axis=0), preferred_element_type=F32)
        ot = ot / jnp.where(row_lo, denoms[0], denoms[1])
        for t, s in enumerate(slabs):
            o_ref[0, row0:row0 + rows, s * LANES:(s + 1) * LANES] = (
                ot[:, t * rows:(t + 1) * rows].T.astype(o_ref.dtype))

    pending = {}
    for i in range(len(units) + ATTN_LOOKAHEAD):
        if i < len(units):
            pending[i] = scores(units[i])
        if i >= ATTN_LOOKAHEAD:
            finish(units[i - ATTN_LOOKAHEAD], pending.pop(i - ATTN_LOOKAHEAD))


def _win_attn_kernel(sink_ref, q_ref, kp_ref, kc_ref, kn_ref, vp_ref, vc_ref, vn_ref,
                     kx_ref, vx_ref, bf_ref, bm_ref, bl_ref, o_ref):
    tq = q_ref.shape[1]
    n_sub = tq // WINDOW
    span = 3 * WINDOW
    q = q_ref[0]
    kloc = _half_variants(jnp.concatenate([kp_ref[0], kc_ref[0], kn_ref[0]], axis=0), 1)
    vloc = _half_variants(jnp.concatenate([vp_ref[0], vc_ref[0], vn_ref[0]], axis=1), 0)
    kctx = _half_variants(kx_ref[0], 1)
    vctx = _half_variants(vx_ref[0], 0)
    jobs = []
    for sb in range(n_sub):
        bias = (bf_ref if sb == 0 else bl_ref if sb == n_sub - 1 else bm_ref)[0]
        ks = {key: val[sb * WINDOW:sb * WINDOW + span] for key, val in kloc.items()}
        vs = {key: val[:, sb * WINDOW:sb * WINDOW + span] for key, val in vloc.items()}
        jobs.append((sb * WINDOW, WINDOW, [ks, kctx], [vs, vctx], [bias, None]))
    _attend(jobs, q, sink_ref, o_ref)


def _band_bias():
    span = 3 * WINDOW
    c = np.arange(span)[:, None]
    r = np.arange(2 * WINDOW)[None, :] % WINDOW
    band = (c >= r) & (c <= r + 2 * WINDOW)
    variants = [band & (c >= WINDOW), band, band & (c < 2 * WINDOW)]
    return jnp.asarray(np.where(np.stack(variants), 0.0, NEG_INF), F32)


def _window_attention(q, k, v, k_ctx, v_ctx, sink, tq):
    b, l, _ = q.shape
    n_ctx = k_ctx.shape[1]
    assert tq % WINDOW == 0 and l % tq == 0 and tq >= 2 * WINDOW
    r = tq // WINDOW
    nwb = l // WINDOW
    nj = l // tq
    kvw = k.shape[2]
    i_prev = lambda j: jnp.maximum(j * r - 1, 0)
    i_next = lambda j: jnp.minimum((j + 1) * r, nwb - 1)
    prev = pl.BlockSpec((1, WINDOW, kvw), lambda bi, j: (bi, i_prev(j), 0))
    cur = pl.BlockSpec((1, tq, kvw), lambda bi, j: (bi, j, 0))
    nxt = pl.BlockSpec((1, WINDOW, kvw), lambda bi, j: (bi, i_next(j), 0))
    ctx = pl.BlockSpec((1, n_ctx, kvw), lambda bi, j: (bi, 0, 0))
    prev_t = pl.BlockSpec((1, kvw, WINDOW), lambda bi, j: (bi, 0, i_prev(j)))
    cur_t = pl.BlockSpec((1, kvw, tq), lambda bi, j: (bi, 0, j))
    nxt_t = pl.BlockSpec((1, kvw, WINDOW), lambda bi, j: (bi, 0, i_next(j)))
    ctx_t = pl.BlockSpec((1, kvw, n_ctx), lambda bi, j: (bi, 0, 0))
    bshape = (1, 3 * WINDOW, 2 * WINDOW)
    bias_first = pl.BlockSpec(bshape, lambda bi, j: (jnp.where(j == 0, 0, 1), 0, 0))
    bias_mid = pl.BlockSpec(bshape, lambda bi, j: (1, 0, 0))
    bias_last = pl.BlockSpec(bshape, lambda bi, j: (jnp.where(j == nj - 1, 2, 1), 0, 0))
    bias = _band_bias()
    return pl.pallas_call(
        _win_attn_kernel,
        grid=(b, nj),
        in_specs=[pl.BlockSpec(memory_space=pltpu.SMEM),
                  pl.BlockSpec((1, tq, ATTN_W), lambda bi, j: (bi, j, 0)),
                  prev, cur, nxt, prev_t, cur_t, nxt_t, ctx, ctx_t, bias_first, bias_mid, bias_last],
        out_specs=pl.BlockSpec((1, tq, ATTN_W), lambda bi, j: (bi, j, 0)),
        out_shape=jax.ShapeDtypeStruct((b, l, ATTN_W), BF16),
        compiler_params=_cparams(2),
        name="window_attention",
    )(sink, q, k, k, k, v, v, v, k_ctx, v_ctx, bias, bias, bias)


def _ctx_attn_kernel(sink_ref, q_ref, k_ref, v_ref, o_ref):
    job = (0, q_ref.shape[1], [_half_variants(k_ref[0], 1)], [_half_variants(v_ref[0], 0)], [None])
    _attend([job], q_ref[0], sink_ref, o_ref)


def _context_attention(q, k, v, sink):
    b, l, _ = q.shape
    kvw = k.shape[2]
    return pl.pallas_call(
        _ctx_attn_kernel,
        grid=(b,),
        in_specs=[pl.BlockSpec(memory_space=pltpu.SMEM),
                  pl.BlockSpec((1, l, ATTN_W), lambda bi: (bi, 0, 0)),
                  pl.BlockSpec((1, l, kvw), lambda bi: (bi, 0, 0)),
                  pl.BlockSpec((1, kvw, l), lambda bi: (bi, 0, 0))],
        out_specs=pl.BlockSpec((1, l, ATTN_W), lambda bi: (bi, 0, 0)),
        out_shape=jax.ShapeDtypeStruct((b, l, ATTN_W), BF16),
        compiler_params=_cparams(1),
        name="context_attention",
    )(sink, q, k, v)


def _taps_kernel(seq_len, z_ref, w1_ref, b1_ref, f1_ref, w2_ref, b2_ref, f2_ref, w3t_ref,
                 dl_ref, o_ref):
    tn = z_ref.shape[1]
    ch = dl_ref.shape[0]
    z = z_ref[...]
    h = jnp.sin(f1_ref[...] * (jnp.dot(w1_ref[...], z, preferred_element_type=F32,
                                       precision=HIGHEST) + b1_ref[...]))
    h = jnp.sin(f2_ref[...] * (jnp.dot(w2_ref[...], h, preferred_element_type=F32,
                                       precision=HIGHEST) + b2_ref[...]))
    full = jnp.dot(w3t_ref[...], h.astype(BF16), preferred_element_type=F32)
    n0 = pl.program_id(0) * tn
    is_fwd = n0 < seq_len
    decay = jnp.exp(-z[0:1, :] * dl_ref[...])
    pos = n0 + lax.broadcasted_iota(jnp.int32, (1, tn), 1)
    for o in range(HY_ORDER):
        fwd = full[o * 2 * ch:o * 2 * ch + ch]
        bwd = full[o * 2 * ch + ch:(o + 1) * 2 * ch]
        taps = jnp.where(is_fwd, fwd, bwd) * decay
        o_ref[o] = jnp.where(pos == seq_len, 0.0, taps).astype(o_ref.dtype)


def _filter_tables(l):
    t = jnp.linspace(0.0, 1.0, l, dtype=F32)[:, None]
    w = (2.0 * math.pi / l) * jnp.arange(l, dtype=F32)[:, None]
    f = jnp.linspace(1e-4, HY_BANDS - 1, HY_BANDS, dtype=F32)[None, :]
    z = jnp.concatenate([t, jnp.cos(f * w), -jnp.sin(f * w)], axis=-1)
    zc = jnp.concatenate([z, z[-1:], jnp.flip(z[1:], axis=0)], axis=0)
    return jnp.pad(zc, ((0, 0), (0, LANES - HY_EMB))).T


def _hyena_taps(l, zct, w1t, b1, f1, w2t, b2, f2, w3t, deltas, layer):
    ch = deltas.shape[0]
    fw = w2t.shape[1]
    tn = min(1024, l)
    return pl.pallas_call(
        functools.partial(_taps_kernel, l),
        grid=(2 * l // tn,),
        in_specs=[pl.BlockSpec((LANES, tn), lambda j: (0, j)),
                  _layer_spec((fw, LANES), layer),
                  _layer_spec((fw, 1), layer),
                  _layer_spec((fw, 1), layer),
                  _layer_spec((fw, fw), layer),
                  _layer_spec((fw, 1), layer),
                  _layer_spec((fw, 1), layer),
                  _layer_spec(w3t.shape[1:], layer),
                  _full_spec((ch, 1))],
        out_specs=pl.BlockSpec((HY_ORDER, ch, tn), lambda j: (0, 0, j)),
        out_shape=jax.ShapeDtypeStruct((HY_ORDER, ch, 2 * l), BF16),
        compiler_params=_cparams(1),
        name="hyena_taps",
    )(zct, w1t, b1, f1, w2t, b2, f2, w3t, deltas)


def _dft_consts(na, nb):
    a = np.arange(na)
    fa = np.exp(-2j * np.pi * np.outer(a, a) / na)
    b = np.arange(nb)
    g = np.exp(-2j * np.pi * np.outer(b, b) / nb)
    fr, fi = fa.real, fa.imag
    gr, gi = g.real, g.imag
    h = na // 2
    m1 = np.block([[fr[:, :h], -fi[:, :h]], [fi[:, :h], fr[:, :h]]])
    m1f = np.concatenate([fr, fi], axis=0)
    m2 = np.block([[gr, gi], [-gi, gr]])
    m2i = np.block([[gr, -gi], [gi, gr]])
    m1i = np.block([[fr[:h], fi[:h]], [-fi[:h], fr[:h]]])
    tw = np.exp(-2j * np.pi * np.outer(a, b) / (na * nb))
    bf = lambda m: jnp.asarray(m, dtype=F32).astype(BF16)
    return dict(m1=bf(m1), m1f=bf(m1f), m2=bf(m2), m2i=bf(m2i), m1i=bf(m1i),
                tr=bf(tw.real), ti=bf(tw.imag))


def _stage2(a_all, tr, ti, m2):
    na, nb = tr.shape
    n = a_all.shape[1] // nb
    lhs = []
    for i in range(n):
        ar = a_all[:na, i * nb:(i + 1) * nb]
        ai = a_all[na:, i * nb:(i + 1) * nb]
        lhs.append(jnp.concatenate([ar * tr - ai * ti, ar * ti + ai * tr], axis=1))
    return jnp.dot(jnp.concatenate(lhs, axis=0), m2, preferred_element_type=F32)


def _filter_spectra(taps, m1f, m2, tr, ti):
    na, nb = tr.shape
    a_all = jnp.dot(m1f, jnp.concatenate(taps, axis=1), preferred_element_type=F32).astype(BF16)
    x = (_stage2(a_all, tr, ti, m2) * (1.0 / (na * nb))).astype(BF16)
    return [x[i * na:(i + 1) * na] for i in range(len(taps))]


def _shift_prev(x):
    rows, nb = x.shape
    lane = lax.broadcasted_iota(jnp.int32, x.shape, 1)
    row = lax.broadcasted_iota(jnp.int32, x.shape, 0)
    r = pltpu.roll(x, 1, axis=1)
    r2 = pltpu.roll(r, 1, axis=0)
    return jnp.where(lane == 0, jnp.where(row == 0, 0.0, r2), r)


def _shift_next(x):
    rows, nb = x.shape
    lane = lax.broadcasted_iota(jnp.int32, x.shape, 1)
    row = lax.broadcasted_iota(jnp.int32, x.shape, 0)
    r = pltpu.roll(x, nb - 1, axis=1)
    r2 = pltpu.roll(r, rows - 1, axis=0)
    return jnp.where(lane == nb - 1, jnp.where(row == rows - 1, 0.0, r2), r)


def _fft_conv(zs, hfs, m1, m2, m2i, m1i, tr, ti):
    na, nb = tr.shape
    n = len(zs)
    h = na // 2
    zc = jnp.concatenate([jnp.concatenate([zr, zi], axis=0) for zr, zi in zs], axis=1)
    a_all = jnp.dot(m1, zc.astype(BF16), preferred_element_type=F32).astype(BF16)
    x_all = _stage2(a_all, tr, ti, m2).astype(BF16)
    ys = []
    for i in range(n):
        x = x_all[i * na:(i + 1) * na]
        xr, xi = x[:, :nb], x[:, nb:]
        hr, hi = hfs[i][:, :nb], hfs[i][:, nb:]
        ys.append(jnp.concatenate([xr * hr - xi * hi, xr * hi + xi * hr], axis=1))
    b_all = jnp.dot(jnp.concatenate(ys, axis=0), m2i, preferred_element_type=F32).astype(BF16)
    rhs = []
    for i in range(n):
        bm = b_all[i * na:(i + 1) * na]
        br, bi = bm[:, :nb], bm[:, nb:]
        rhs.append(jnp.concatenate([br * tr + bi * ti, bi * tr - br * ti], axis=0))
    yo = jnp.dot(m1i, jnp.concatenate(rhs, axis=1), preferred_element_type=F32)
    return [(yo[:h, i * nb:(i + 1) * nb], yo[h:, i * nb:(i + 1) * nb]) for i in range(n)]


def _hyena_kernel(sw_ref, sb_ref, hb_ref, x1_ref, x2_ref, z_ref, taps_ref,
                  m1_ref, m1f_ref, m2_ref, m2i_ref, m1i_ref, tr_ref, ti_ref, o_ref):
    nbatch, ct = x1_ref.shape[:2]
    nch = hb_ref.shape[1]
    tr, ti = tr_ref[...], ti_ref[...]
    mats = (m1_ref[...], m2_ref[...], m2i_ref[...], m1i_ref[...], tr, ti)
    gates = (x1_ref, x2_ref)
    c0 = pl.program_id(0) * ct

    def short(ref, stream, b, c):
        ch = stream * nch + c0 + c
        x = ref[b, c].astype(F32)
        return (sw_ref[0, ch] * _shift_prev(x) + sw_ref[1, ch] * x
                + sw_ref[2, ch] * _shift_next(x) + sb_ref[0, ch])

    specs = _filter_spectra([taps_ref[o, c] for o in range(HY_ORDER) for c in range(ct)],
                            m1f_ref[...], m2_ref[...], tr, ti)
    seqs = [(c, p) for c in range(ct) for p in range(nbatch // 2)]
    zz = [(short(z_ref, 2, 2 * p, c), short(z_ref, 2, 2 * p + 1, c)) for c, p in seqs]
    for o in range(HY_ORDER):
        ys = _fft_conv(zz, [specs[o * ct + c] for c, _ in seqs], *mats)
        zz = [tuple(short(gates[o], o, 2 * p + q, c) * (ys[i][q] + hb_ref[o, c0 + c] * zz[i][q])
                    for q in range(2)) for i, (c, p) in enumerate(seqs)]
    for i, (c, p) in enumerate(seqs):
        for q in range(2):
            o_ref[2 * p + q, c] = zz[i][q].astype(o_ref.dtype)


def _hyena_long(hyt, taps, short_w, short_b, hy_bias, dc, ct):
    b, c3, l = hyt.shape
    ch = c3 // 3
    nb = LANES
    na = 2 * l // nb
    assert b % 2 == 0 and ct % 2 == 0 and ch % ct == 0
    h4 = hyt.reshape(b, c3, na // 2, nb)
    t4 = taps.reshape(HY_ORDER, ch, na, nb)
    blk = (b, ct, na // 2, nb)
    nct = ch // ct
    smem = pl.BlockSpec(memory_space=pltpu.SMEM)
    out = pl.pallas_call(
        _hyena_kernel,
        grid=(nct,),
        in_specs=[smem, smem, smem,
                  pl.BlockSpec(blk, lambda j: (0, j, 0, 0)),
                  pl.BlockSpec(blk, lambda j: (0, nct + j, 0, 0)),
                  pl.BlockSpec(blk, lambda j: (0, 2 * nct + j, 0, 0)),
                  pl.BlockSpec((HY_ORDER, ct, na, nb), lambda j: (0, j, 0, 0)),
                  _full_spec((2 * na, na)), _full_spec((2 * na, na)),
                  _full_spec((2 * nb, 2 * nb)), _full_spec((2 * nb, 2 * nb)),
                  _full_spec((na, 2 * na)), _full_spec((na, nb)), _full_spec((na, nb))],
        out_specs=pl.BlockSpec(blk, lambda j: (0, j, 0, 0)),
        out_shape=jax.ShapeDtypeStruct((b, ch, na // 2, nb), BF16),
        compiler_params=_cparams(1),
        name="hyena_long",
    )(short_w, short_b, hy_bias, h4, h4, h4, t4,
      dc["m1"], dc["m1f"], dc["m2"], dc["m2i"], dc["m1i"], dc["tr"], dc["ti"])
    return out.reshape(b, ch, l)


def _dense_dft_consts(l):
    n = 2 * l
    k = np.arange(n)
    ang = 2 * np.pi * np.outer(k, k) / n
    cos, sin = np.cos(ang), np.sin(ang)
    fwd = np.concatenate([cos, -sin], axis=1)
    inv = np.concatenate([cos[:, :l], -sin[:, :l]], axis=0) / n
    bf = lambda m: jnp.asarray(m, dtype=F32).astype(BF16)
    return dict(fwd_half=bf(fwd[:l]), fwd=bf(fwd), inv=bf(inv))


def _hyena_short_kernel(sw_ref, sb_ref, hb_ref, x1_ref, x2_ref, z_ref, taps_ref,
                        fh_ref, ff_ref, inv_ref, o_ref):
    nbatch, ct, l = x1_ref.shape
    n = 2 * l
    lane = lax.broadcasted_iota(jnp.int32, (ct, l), 1)

    def short(ref, stream, b):
        x = ref[b].astype(F32)
        prev = jnp.where(lane == 0, 0.0, pltpu.roll(x, 1, axis=1))
        nxt = jnp.where(lane == l - 1, 0.0, pltpu.roll(x, l - 1, axis=1))
        return sw_ref[stream, 0] * prev + sw_ref[stream, 1] * x + sw_ref[stream, 2] * nxt + sb_ref[stream]

    gates = (x1_ref, x2_ref)
    zz = [short(z_ref, 2, b) for b in range(nbatch)]
    for o in range(HY_ORDER):
        hf = jnp.dot(taps_ref[o].astype(BF16), ff_ref[...], preferred_element_type=F32)
        hr, hi = hf[:, :n], hf[:, n:]
        x_all = jnp.dot(jnp.concatenate(zz, axis=0).astype(BF16), fh_ref[...],
                        preferred_element_type=F32)
        ys = []
        for b in range(nbatch):
            x = x_all[b * ct:(b + 1) * ct]
            xr, xi = x[:, :n], x[:, n:]
            ys.append(jnp.concatenate([xr * hr - xi * hi, xr * hi + xi * hr], axis=1).astype(BF16))
        conv = jnp.dot(jnp.concatenate(ys, axis=0), inv_ref[...], preferred_element_type=F32)
        zz = [short(gates[o], o, b) * (conv[b * ct:(b + 1) * ct] + hb_ref[o] * zz[b])
              for b in range(nbatch)]
    for b in range(nbatch):
        o_ref[b] = zz[b].astype(o_ref.dtype)


def _hyena_short(hyt, taps, short_w3, short_b3, hy_bias3, dd, ct):
    b, c3, l = hyt.shape
    ch = c3 // 3
    n = 2 * l
    nct = ch // ct
    blk = (b, ct, l)
    return pl.pallas_call(
        _hyena_short_kernel,
        grid=(nct,),
        in_specs=[pl.BlockSpec((3, 3, ct, 1), lambda j: (0, 0, j, 0)),
                  pl.BlockSpec((3, ct, 1), lambda j: (0, j, 0)),
                  pl.BlockSpec((HY_ORDER, ct, 1), lambda j: (0, j, 0)),
                  pl.BlockSpec(blk, lambda j: (0, j, 0)),
                  pl.BlockSpec(blk, lambda j: (0, nct + j, 0)),
                  pl.BlockSpec(blk, lambda j: (0, 2 * nct + j, 0)),
                  pl.BlockSpec((HY_ORDER, ct, n), lambda j: (0, j, 0)),
                  _full_spec((l, 2 * n)), _full_spec((n, 2 * n)), _full_spec((2 * n, l))],
        out_specs=pl.BlockSpec(blk, lambda j: (0, j, 0)),
        out_shape=jax.ShapeDtypeStruct((b, ch, l), BF16),
        compiler_params=_cparams(1),
        name="hyena_short",
    )(short_w3, short_b3, hy_bias3, hyt, hyt, hyt, taps, dd["fwd_half"], dd["fwd"], dd["inv"])


def _mixout_kernel(a_ref, hy_ref, up_ref, uc_ref, un_ref, x_ref, m_ref, g_ref, wa_ref, wh_ref, wc_ref,
                   dw_ref, db_ref, lg_ref, lb_ref, o_ref, buf, cv_buf):
    tm = uc_ref.shape[1]
    ch = cv_buf.shape[1]
    j = pl.program_id(1)
    nj = pl.num_programs(1)
    y = jnp.dot(a_ref[0], wa_ref[...], preferred_element_type=F32)
    y = y + pl.dot(hy_ref[0], wh_ref[...], trans_a=True)

    def glu(u):
        return u.astype(F32)

    buf[0, 0:HALO] = jnp.where(j > 0, glu(up_ref[0]), 0.0)
    buf[0, HALO:HALO + tm] = glu(uc_ref[0])
    buf[0, HALO + tm:] = jnp.where(j < nj - 1, glu(un_ref[0]), 0.0)
    span = tm + 2 * HALO - SUBLANES
    for s in range(1, SUBLANES):
        buf[s, 0:span] = buf[0, pl.ds(s, span), :]
    pad = (CV_K - 1) // 2
    rc = 128
    for r0 in range(0, tm, rc):
        acc = jnp.zeros((rc, ch), F32) + db_ref[...]
        for k in range(CV_K):
            off = HALO - pad + k
            acc = acc + dw_ref[k:k + 1, :] * buf[off % SUBLANES, pl.ds(r0 + off - off % SUBLANES, rc), :]
        mu = jnp.mean(acc, axis=-1, keepdims=True)
        cen = acc - mu
        var = jnp.mean(cen * cen, axis=-1, keepdims=True)
        cv = cen * lax.rsqrt(var + EPS) * lg_ref[...] + lb_ref[...]
        cv_buf[r0:r0 + rc] = _silu(cv).astype(cv_buf.dtype)
    y = y + jnp.dot(cv_buf[...], wc_ref[...], preferred_element_type=F32)
    o_ref[0] = x_ref[0] + _rms(y, m_ref[0, 2:3, :] * g_ref[...])


def _mixout(attn, hyt, u, x, mods, mod_row, g_post, wa, wh, wc, dw_w, dw_b, ln_g, ln_b, layer, tm):
    b, l, d = x.shape
    c2 = u.shape[2]
    ch = c2
    r = tm // HALO
    nh = l // HALO
    mrow = (lambda bi: bi) if mod_row is None else (lambda bi: mod_row)
    return pl.pallas_call(
        _mixout_kernel,
        grid=(b, l // tm),
        in_specs=[pl.BlockSpec((1, tm, attn.shape[2]), lambda bi, j: (bi, j, 0)),
                  pl.BlockSpec((1, hyt.shape[1], tm), lambda bi, j: (bi, 0, j)),
                  pl.BlockSpec((1, HALO, c2), lambda bi, j: (bi, jnp.maximum(j * r - 1, 0), 0)),
                  pl.BlockSpec((1, tm, c2), lambda bi, j: (bi, j, 0)),
                  pl.BlockSpec((1, HALO, c2), lambda bi, j: (bi, jnp.minimum((j + 1) * r, nh - 1), 0)),
                  pl.BlockSpec((1, tm, d), lambda bi, j: (bi, j, 0)),
                  pl.BlockSpec((1, 6, d), lambda bi, j: (mrow(bi), 0, 0)),
                  _layer_spec((1, d), layer),
                  _layer_spec(wa.shape[1:], layer),
                  _layer_spec(wh.shape[1:], layer),
                  _layer_spec(wc.shape[1:], layer),
                  _layer_spec((CV_K, ch), layer),
                  _layer_spec((1, ch), layer), _layer_spec((1, ch), layer), _layer_spec((1, ch), layer)],
        out_specs=pl.BlockSpec((1, tm, d), lambda bi, j: (bi, j, 0)),
        out_shape=jax.ShapeDtypeStruct((b, l, d), F32),
        scratch_shapes=[pltpu.VMEM((SUBLANES, tm + 2 * HALO, ch), F32), pltpu.VMEM((tm, ch), BF16)],
        compiler_params=_cparams(2),
        name="mix_outproj",
    )(attn, hyt, u, u, u, x, mods, g_post, wa, wh, wc, dw_w, dw_b, ln_g, ln_b)


def _ffn_kernel(n_chunks, xp_ref, xc_ref, xn_ref, m_ref, gpre_ref, gpost_ref, wu_ref, wg_ref,
                dw_ref, db_ref, wd_ref, o_ref):
    tm = xc_ref.shape[1]
    dff = wu_ref.shape[1]
    cw = dff // n_chunks
    j = pl.program_id(1)
    nj = pl.num_programs(1)
    shift, scale, gate = m_ref[0, 3:4, :], m_ref[0, 4:5, :], m_ref[0, 5:6, :]
    gpre = gpre_ref[...]

    def prep(x):
        return (_rms(x, gpre) * (1.0 + scale) + shift).astype(BF16)

    xc = xc_ref[0]
    hb = prep(xc)
    hext = jnp.concatenate([prep(xp_ref[0]), hb, prep(xn_ref[0])], axis=0)
    y = jnp.zeros((tm, o_ref.shape[2]), F32)
    for c in range(n_chunks):
        cols = slice(c * cw, (c + 1) * cw)
        g = jnp.dot(hext, wg_ref[:, cols], preferred_element_type=F32)
        u = jnp.dot(hb, wu_ref[:, cols], preferred_element_type=F32)
        g = jnp.concatenate([jnp.where(j > 0, g[:HALO], 0.0), g[HALO:HALO + tm],
                             jnp.where(j < nj - 1, g[HALO + tm:], 0.0)], axis=0)
        rows = tm + 2 * HALO
        g_prev = pltpu.roll(g, 1, axis=0)[HALO:HALO + tm]
        g_next = pltpu.roll(g, rows - 1, axis=0)[HALO:HALO + tm]
        conv = (dw_ref[0:1, cols] * g_prev + dw_ref[1:2, cols] * g[HALO:HALO + tm]
                + dw_ref[2:3, cols] * g_next + db_ref[:, cols])
        act = (_silu(conv) * u).astype(BF16)
        y = y + jnp.dot(act, wd_ref[cols, :], preferred_element_type=F32)
    o_ref[0] = xc + _rms(y, gate * gpost_ref[...])


def _ffn(x, mods, mod_row, g_pre, g_post, wu, wg, dw_w, dw_b, wd, layer, tm, n_chunks):
    b, l, d = x.shape
    dff = wu.shape[2]
    r = tm // HALO
    nh = l // HALO
    mrow = (lambda bi: bi) if mod_row is None else (lambda bi: mod_row)
    return pl.pallas_call(
        functools.partial(_ffn_kernel, n_chunks),
        grid=(b, l // tm),
        in_specs=[pl.BlockSpec((1, HALO, d), lambda bi, j: (bi, jnp.maximum(j * r - 1, 0), 0)),
                  pl.BlockSpec((1, tm, d), lambda bi, j: (bi, j, 0)),
                  pl.BlockSpec((1, HALO, d), lambda bi, j: (bi, jnp.minimum((j + 1) * r, nh - 1), 0)),
                  pl.BlockSpec((1, 6, d), lambda bi, j: (mrow(bi), 0, 0)),
                  _layer_spec((1, d), layer), _layer_spec((1, d), layer),
                  _layer_spec((d, dff), layer), _layer_spec((d, dff), layer),
                  _layer_spec((FFN_K, dff), layer), _layer_spec((1, dff), layer),
                  _layer_spec((dff, d), layer)],
        out_specs=pl.BlockSpec((1, tm, d), lambda bi, j: (bi, j, 0)),
        out_shape=jax.ShapeDtypeStruct((b, l, d), F32),
        compiler_params=_cparams(2),
        name="conv_ffn",
    )(x, x, x, mods, g_pre, g_post, wu, wg, dw_w, dw_b, wd)


def _rope_tables(l):
    rows = l // GRID_W
    row = jnp.repeat(jnp.arange(rows, dtype=F32), GRID_W)
    col = jnp.tile(jnp.arange(GRID_W, dtype=F32), rows)
    inv = ROPE_THETA ** (-jnp.arange(0, AXIS_ROT, 2, dtype=F32) / AXIS_ROT)
    ar, ac = row[:, None] * inv, col[:, None] * inv
    cos = jnp.concatenate([jnp.cos(ar), jnp.cos(ar), jnp.cos(ac), jnp.cos(ac)], axis=1)
    sin = jnp.concatenate([-jnp.sin(ar), jnp.sin(ar), -jnp.sin(ac), jnp.sin(ac)], axis=1)
    rep = LANES // HEAD_DIM
    return jnp.tile(cos, (1, rep)), jnp.tile(sin, (1, rep))


def kernel(x, c, ctx, c_ctx, w_mod, b_mod, g_pre_mix, g_post_mix, g_pre_ffn, g_post_ffn, w_in, attn_sink, hy_short_w, hy_short_b, hy_w1, hy_b1, hy_freq1, hy_w2, hy_b2, hy_freq2, hy_w3, hy_bias, cv_dw_w, cv_dw_b, cv_ln_g, cv_ln_b, w_out, w_up, ffn_dw_w, ffn_dw_b, w_down):
    b, l, d = x.shape
    lc = ctx.shape[1]
    depth = w_in.shape[0]
    hy_ch = hy_bias.shape[2]
    cv_ch = cv_dw_w.shape[2]
    dff = w_down.shape[1]
    k0 = ATTN_W
    hy0 = k0 + 2 * KV_W
    cv0 = hy0 + (HY_ORDER + 1) * hy_ch
    fw = hy_w2.shape[1]

    v0 = k0 + KV_W
    wqkv = w_in[:, :, :k0].astype(BF16)
    whyt = jnp.swapaxes(w_in[:, :, k0:cv0], 1, 2).astype(BF16)
    wcv = w_in[:, :, cv0:].astype(BF16)
    wo_a = w_out[:, :ATTN_W].astype(BF16)
    wo_h = w_out[:, ATTN_W:ATTN_W + hy_ch].astype(BF16)
    wo_c = w_out[:, ATTN_W + hy_ch:].astype(BF16)
    wu = w_up[:, :, :dff].astype(BF16)
    wg = w_up[:, :, dff:].astype(BF16)
    wd = w_down.astype(BF16)
    row = lambda a: a[:, None, :]
    gpm, gqm, gpf, gqf = row(g_pre_mix), row(g_post_mix), row(g_pre_ffn), row(g_post_ffn)
    col = lambda a: a[:, :, None]
    w1t = jnp.swapaxes(jnp.pad(hy_w1, ((0, 0), (0, LANES - HY_EMB), (0, 0))), 1, 2)
    w3t = jnp.swapaxes(hy_w3, 1, 2).astype(BF16)
    deltas = jnp.abs(jnp.linspace(math.log(HY_TARGET) / HY_SLOW, math.log(HY_TARGET) / HY_FAST,
                                  hy_ch, dtype=F32))[:, None]
    short_b2 = row(hy_short_b)
    sw5 = jnp.transpose(hy_short_w.reshape(depth, 3, 3, hy_ch), (0, 2, 1, 3))[..., None]
    sb4 = hy_short_b.reshape(depth, 3, hy_ch)[..., None]
    hb4 = hy_bias[..., None]

    n_rows = 8 * ((b + 1 + 7) // 8)
    cond = jnp.zeros((n_rows, d), F32).at[:b].set(c).at[b].set(c_ctx)
    mods = _modulation(cond, w_mod, b_mod).reshape(depth, n_rows, 6, d)

    cs, sn = _rope_tables(l)
    tab_l, tab_c = _filter_tables(l), _filter_tables(lc)
    dc = _dft_consts(2 * l // LANES, LANES)
    dd = _dense_dft_consts(lc)
    tm_in, tm_mix, tm_ffn, tq, hy_ct = min(1024, l), min(1024, l), min(512, l), min(2048, l), 16
    tmc = lc

    for i in range(depth):
        last = i == depth - 1
        m_i = mods[i]
        hyp = (w1t, col(hy_b1), col(hy_freq1), jnp.swapaxes(hy_w2, 1, 2), col(hy_b2), col(hy_freq2),
               w3t, deltas)

        q, k, v, hyt, cvu = _inproj(x, m_i, None, gpm, wqkv, whyt, wcv, i, cs, sn, tm_in)
        qc, kc, vc, hyt_c, cvu_c = _inproj(ctx, m_i, b, gpm, wqkv, whyt, wcv, i, None, None, tmc)

        attn = _window_attention(q, k, v, kc, vc, attn_sink[i][None, :], tq)
        taps = _hyena_taps(l, tab_l, *hyp, i)
        hy = _hyena_long(hyt, taps, hy_short_w[i], short_b2[i], hy_bias[i], dc, hy_ct)
        cvp = (cv_dw_w, row(cv_dw_b), row(cv_ln_g), row(cv_ln_b))
        x = _mixout(attn, hy, cvu, x, m_i, None, gqm, wo_a, wo_h, wo_c, *cvp, i, tm_mix)
        x = _ffn(x, m_i, None, gpf, gqf, wu, wg, ffn_dw_w, row(ffn_dw_b), wd, i, tm_ffn, 1)

        if not last:
            attn_c = _context_attention(qc, kc, vc, attn_sink[i][None, :])
            taps_c = _hyena_taps(lc, tab_c, *hyp, i)
            hy_c = _hyena_short(hyt_c, taps_c, sw5[i], sb4[i], hb4[i], dd, 32)
            ctx = _mixout(attn_c, hy_c, cvu_c, ctx, m_i, b, gqm, wo_a, wo_h, wo_c, *cvp, i, tmc)
            ctx = _ffn(ctx, m_i, b, gpf, gqf, wu, wg, ffn_dw_w, row(ffn_dw_b), wd, i, tmc, 1)
    return x
```

```python
import functools
import math

import numpy as np
import jax
import jax.numpy as jnp
from jax import lax
from jax.experimental import pallas as pl
from jax.experimental.pallas import tpu as pltpu

F32 = jnp.float32
BF16 = jnp.bfloat16
HIGHEST = lax.Precision.HIGHEST

N_HEADS = 8
N_KV_HEADS = 2
HEAD_DIM = 64
GROUP = N_HEADS // N_KV_HEADS
ATTN_W = N_HEADS * HEAD_DIM
KV_W = N_KV_HEADS * HEAD_DIM
WINDOW = 128
GRID_W = 64
ROPE_THETA = 10000.0
AXIS_ROT = HEAD_DIM // 2
HY_ORDER = 2
HY_EMB = 33
HY_BANDS = (HY_EMB - 1) // 2
HY_TARGET = 1e-2
HY_FAST = 0.3
HY_SLOW = 1.5
CV_K = 31
FFN_K = 3
EPS = 1e-6
NEG_INF = -1e30
LOG2E = math.log2(math.e)

LANES = 128
SUBLANES = 8
HALO = 16
ATTN_LOOKAHEAD = 2
VMEM_LIMIT = 56 * 1024 * 1024

NT = (((1,), (1,)), ((), ()))


def _cparams(n_axes):
    return pltpu.CompilerParams(dimension_semantics=("parallel",) * n_axes,
                                vmem_limit_bytes=VMEM_LIMIT)


def _full_spec(shape):
    nd = len(shape)
    return pl.BlockSpec(shape, lambda *_: (0,) * nd, pipeline_mode=pl.Buffered(1))


def _layer_spec(shape, layer):
    nd = len(shape)
    return pl.BlockSpec((None,) + tuple(shape), lambda *_: (layer,) + (0,) * nd,
                        pipeline_mode=pl.Buffered(1))


def _rms(t, g):
    return t * lax.rsqrt(jnp.mean(t * t, axis=-1, keepdims=True) + EPS) * g


def _silu(t):
    return t * jax.nn.sigmoid(t)


def _mod_kernel(c_ref, w_ref, b_ref, o_ref):
    s = _silu(c_ref[...])
    o_ref[0] = jnp.dot(s, w_ref[0], preferred_element_type=F32, precision=HIGHEST) + b_ref[0]


def _modulation(cond, w_mod, b_mod):
    depth, d, n = w_mod.shape
    rows = cond.shape[0]
    tn = 1536
    assert n % tn == 0
    return pl.pallas_call(
        _mod_kernel,
        grid=(depth, n // tn),
        in_specs=[pl.BlockSpec((rows, d), lambda i, j: (0, 0)),
                  pl.BlockSpec((1, d, tn), lambda i, j: (i, 0, j)),
                  pl.BlockSpec((1, 1, tn), lambda i, j: (i, 0, j))],
        out_specs=pl.BlockSpec((1, rows, tn), lambda i, j: (i, 0, j)),
        out_shape=jax.ShapeDtypeStruct((depth, rows, n), F32),
        compiler_params=_cparams(2),
        name="modulation",
    )(cond, w_mod, b_mod.reshape(depth, 1, n))


def _swap_halves(t):
    w = t.shape[-1]
    lane = lax.broadcasted_iota(jnp.int32, t.shape, 1)
    from_hi = pltpu.roll(t, w - AXIS_ROT // 2, axis=1)
    from_lo = pltpu.roll(t, AXIS_ROT // 2, axis=1)
    return jnp.where(lane % AXIS_ROT < AXIS_ROT // 2, from_hi, from_lo)


def _inproj_kernel(rope, xp_ref, x_ref, xn_ref, m_ref, g_ref, wqk_ref, wt_ref, wcv_ref, sw_ref, sb_ref,
                   cs_ref, sn_ref, q_ref, k_ref, vt_ref, hy_ref, cv_ref):
    tm = x_ref.shape[1]
    j = pl.program_id(1)
    nj = pl.num_programs(1)

    def prep(x):
        return (_rms(x, g_ref[...]) * (1.0 + m_ref[0, 1:2, :]) + m_ref[0, 0:1, :]).astype(BF16)

    hb = prep(x_ref[0])
    halo = jnp.concatenate([prep(xp_ref[0]), prep(xn_ref[0])], axis=0)
    if rope:
        cs = cs_ref[...]
        sn = sn_ref[...]
    pt = lax.dot_general(wt_ref[...], hb, NT, preferred_element_type=F32)
    ph = lax.dot_general(wt_ref[2 * KV_W:, :], halo, NT, preferred_element_type=F32)
    k = pt[:KV_W].T
    if rope:
        k = k * cs + _swap_halves(k) * sn
    k_ref[0] = jnp.concatenate([k, pltpu.roll(k, HEAD_DIM, axis=1)], axis=1).astype(BF16)
    vt = pt[KV_W:2 * KV_W]
    vt_ref[0] = jnp.concatenate([vt, vt[HEAD_DIM:], vt[:HEAD_DIM]], axis=0).astype(BF16)
    hy = pt[2 * KV_W:]
    col_prev = jnp.where(j > 0, ph[:, HALO - 1:HALO], 0.0)
    col_next = jnp.where(j < nj - 1, ph[:, HALO:HALO + 1], 0.0)
    lane = lax.broadcasted_iota(jnp.int32, (1, LANES), 1)
    prev = pltpu.roll(hy, 1, axis=1)
    nxt = pltpu.roll(hy, tm - 1, axis=1)
    first = jnp.where(lane == 0, col_prev, prev[:, :LANES])
    last = jnp.where(lane == LANES - 1, col_next, nxt[:, tm - LANES:])
    if tm > LANES:
        prev = jnp.concatenate([first, prev[:, LANES:]], axis=1)
        nxt = jnp.concatenate([nxt[:, :tm - LANES], last], axis=1)
    else:
        prev, nxt = first, last
    hy_ref[0] = (sw_ref[0] * prev + sw_ref[1] * hy + sw_ref[2] * nxt + sb_ref[...]).astype(BF16)
    q = jnp.dot(hb, wqk_ref[...], preferred_element_type=F32)
    if rope:
        cq = jnp.concatenate([cs] * (ATTN_W // LANES), axis=1)
        sq = jnp.concatenate([sn] * (ATTN_W // LANES), axis=1)
        q = q * cq + _swap_halves(q) * sq
    q_ref[0] = (q * (HEAD_DIM ** -0.5 * LOG2E)).astype(BF16)
    u = jnp.dot(hb, wcv_ref[...], preferred_element_type=F32)
    ch = u.shape[1] // 2
    cv_ref[0] = (u[:, :ch] * jax.nn.sigmoid(u[:, ch:])).astype(BF16)


def _inproj(x, mods, mod_row, g_pre, wqkv, whyt, wcv, sw, sb, layer, cs, sn, tm):
    b, l, d = x.shape
    r = tm // HALO
    nh = l // HALO
    rope = cs is not None
    if not rope:
        cs = jnp.zeros((l, LANES), F32)
        sn = cs
    nhy = whyt.shape[1] - 2 * KV_W
    ncv = wcv.shape[2] // 2
    mrow = (lambda bi: bi) if mod_row is None else (lambda bi: mod_row)
    return pl.pallas_call(
        functools.partial(_inproj_kernel, rope),
        grid=(b, l // tm),
        in_specs=[pl.BlockSpec((1, HALO, d), lambda bi, j: (bi, jnp.maximum(j * r - 1, 0), 0)),
                  pl.BlockSpec((1, tm, d), lambda bi, j: (bi, j, 0)),
                  pl.BlockSpec((1, HALO, d), lambda bi, j: (bi, jnp.minimum((j + 1) * r, nh - 1), 0)),
                  pl.BlockSpec((1, 6, d), lambda bi, j: (mrow(bi), 0, 0)),
                  _layer_spec((1, d), layer),
                  _layer_spec(wqkv.shape[1:], layer),
                  _layer_spec(whyt.shape[1:], layer),
                  _layer_spec(wcv.shape[1:], layer),
                  _layer_spec(sw.shape[1:], layer),
                  _layer_spec(sb.shape[1:], layer),
                  pl.BlockSpec((tm, LANES), lambda bi, j: (j, 0)),
                  pl.BlockSpec((tm, LANES), lambda bi, j: (j, 0))],
        out_specs=[pl.BlockSpec((1, tm, ATTN_W), lambda bi, j: (bi, j, 0)),
                   pl.BlockSpec((1, tm, 2 * KV_W), lambda bi, j: (bi, j, 0)),
                   pl.BlockSpec((1, 2 * KV_W, tm), lambda bi, j: (bi, 0, j)),
                   pl.BlockSpec((1, nhy, tm), lambda bi, j: (bi, 0, j)),
                   pl.BlockSpec((1, tm, ncv), lambda bi, j: (bi, j, 0))],
        out_shape=[jax.ShapeDtypeStruct((b, l, ATTN_W), BF16),
                   jax.ShapeDtypeStruct((b, l, 2 * KV_W), BF16),
                   jax.ShapeDtypeStruct((b, 2 * KV_W, l), BF16),
                   jax.ShapeDtypeStruct((b, nhy, l), BF16),
                   jax.ShapeDtypeStruct((b, l, ncv), BF16)],
        compiler_params=_cparams(2),
        name="inproj",
    )(x, x, x, mods, g_pre, wqkv, whyt, wcv, sw, sb, cs, sn)


def _half_variants(a, axis):
    idx = lax.broadcasted_iota(jnp.int32, (LANES, 1) if axis == 0 else (1, LANES), axis)
    lo = idx < HEAD_DIM
    x, y = (a[:LANES], a[LANES:]) if axis == 0 else (a[:, :LANES], a[:, LANES:])
    zero = jnp.zeros_like(x)
    return {(0, 0): jnp.where(lo, x, zero), (0, 1): jnp.where(lo, zero, y),
            (1, 0): jnp.where(lo, y, zero), (1, 1): jnp.where(lo, zero, x)}


def _attend(jobs, q, sink_ref, o_ref):
    row_lo = lax.broadcasted_iota(jnp.int32, (LANES, 1), 0) < HEAD_DIM
    units = [(job, kv) for job in jobs for kv in range(N_KV_HEADS)]

    def scores(unit):
        (row0, rows, kvars, _, _), kv = unit
        qs = jnp.concatenate([q[row0:row0 + rows, s * LANES:(s + 1) * LANES]
                              for s in (2 * kv, 2 * kv + 1)], axis=0)
        kcat = jnp.concatenate([kd[(kv, half)] for half in range(2) for kd in kvars], axis=0)
        return lax.dot_general(kcat, qs, NT, preferred_element_type=F32)

    def finish(unit, s_all):
        (row0, rows, kvars, vtvars, masks), kv = unit
        slabs = (2 * kv, 2 * kv + 1)
        vtcat = jnp.concatenate([vd[(kv, half)] for half in range(2) for vd in vtvars], axis=1)
        probs, denoms = [], []
        off = 0
        for half in range(2):
            sink = jnp.concatenate([jnp.full((1, rows), sink_ref[0, 2 * s + half] * LOG2E, F32)
                                    for s in slabs], axis=1)
            m = sink
            segs = []
            for kd, mk in zip(kvars, masks):
                n = kd[(kv, half)].shape[0]
                s = s_all[off:off + n]
                off += n
                if mk is not None:
                    s = s + mk
                segs.append(s)
                m = jnp.maximum(m, jnp.max(s, axis=0, keepdims=True))
            denom = jnp.exp2(sink - m)
            for s in segs:
                p = jnp.exp2(s - m)
                denom = denom + jnp.sum(p, axis=0, keepdims=True)
                probs.append(p.astype(BF16))
            denoms.append(denom)
        ot = jnp.dot(vtcat, jnp.concatenate(probs, axis=0), preferred_element_type=F32)
        ot = ot / jnp.where(row_lo, denoms[0], denoms[1])
        for t, s in enumerate(slabs):
            o_ref[0, row0:row0 + rows, s * LANES:(s + 1) * LANES] = (
                ot[:, t * rows:(t + 1) * rows].T.astype(o_ref.dtype))

    pending = {}
    for i in range(len(units) + ATTN_LOOKAHEAD):
        if i < len(units):
            pending[i] = scores(units[i])
        if i >= ATTN_LOOKAHEAD:
            finish(units[i - ATTN_LOOKAHEAD], pending.pop(i - ATTN_LOOKAHEAD))


def _win_attn_kernel(sink_ref, q_ref, kp_ref, kc_ref, kn_ref, vp_ref, vc_ref, vn_ref,
                     kx_ref, vx_ref, bf_ref, bm_ref, bl_ref, o_ref):
    tq = q_ref.shape[1]
    n_sub = tq // WINDOW
    span = 3 * WINDOW
    q = q_ref[0]
    kloc = _half_variants(jnp.concatenate([kp_ref[0], kc_ref[0], kn_ref[0]], axis=0), 1)
    vloc = _half_variants(jnp.concatenate([vp_ref[0], vc_ref[0], vn_ref[0]], axis=1), 0)
    kctx = _half_variants(kx_ref[0], 1)
    vctx = _half_variants(vx_ref[0], 0)
    jobs = []
    for sb in range(n_sub):
        bias = (bf_ref if sb == 0 else bl_ref if sb == n_sub - 1 else bm_ref)[0]
        ks = {key: val[sb * WINDOW:sb * WINDOW + span] for key, val in kloc.items()}
        vs = {key: val[:, sb * WINDOW:sb * WINDOW + span] for key, val in vloc.items()}
        jobs.append((sb * WINDOW, WINDOW, [ks, kctx], [vs, vctx], [bias, None]))
    _attend(jobs, q, sink_ref, o_ref)


def _band_bias():
    span = 3 * WINDOW
    c = np.arange(span)[:, None]
    r = np.arange(2 * WINDOW)[None, :] % WINDOW
    band = (c >= r) & (c <= r + 2 * WINDOW)
    variants = [band & (c >= WINDOW), band, band & (c < 2 * WINDOW)]
    return jnp.asarray(np.where(np.stack(variants), 0.0, NEG_INF), F32)


def _window_attention(q, k, v, k_ctx, v_ctx, sink, tq):
    b, l, _ = q.shape
    n_ctx = k_ctx.shape[1]
    assert tq % WINDOW == 0 and l % tq == 0 and tq >= 2 * WINDOW
    r = tq // WINDOW
    nwb = l // WINDOW
    nj = l // tq
    kvw = k.shape[2]
    i_prev = lambda j: jnp.maximum(j * r - 1, 0)
    i_next = lambda j: jnp.minimum((j + 1) * r, nwb - 1)
    prev = pl.BlockSpec((1, WINDOW, kvw), lambda bi, j: (bi, i_prev(j), 0))
    cur = pl.BlockSpec((1, tq, kvw), lambda bi, j: (bi, j, 0))
    nxt = pl.BlockSpec((1, WINDOW, kvw), lambda bi, j: (bi, i_next(j), 0))
    ctx = pl.BlockSpec((1, n_ctx, kvw), lambda bi, j: (bi, 0, 0))
    prev_t = pl.BlockSpec((1, kvw, WINDOW), lambda bi, j: (bi, 0, i_prev(j)))
    cur_t = pl.BlockSpec((1, kvw, tq), lambda bi, j: (bi, 0, j))
    nxt_t = pl.BlockSpec((1, kvw, WINDOW), lambda bi, j: (bi, 0, i_next(j)))
    ctx_t = pl.BlockSpec((1, kvw, n_ctx), lambda bi, j: (bi, 0, 0))
    bshape = (1, 3 * WINDOW, 2 * WINDOW)
    bias_first = pl.BlockSpec(bshape, lambda bi, j: (jnp.where(j == 0, 0, 1), 0, 0))
    bias_mid = pl.BlockSpec(bshape, lambda bi, j: (1, 0, 0))
    bias_last = pl.BlockSpec(bshape, lambda bi, j: (jnp.where(j == nj - 1, 2, 1), 0, 0))
    bias = _band_bias()
    return pl.pallas_call(
        _win_attn_kernel,
        grid=(b, nj),
        in_specs=[pl.BlockSpec(memory_space=pltpu.SMEM),
                  pl.BlockSpec((1, tq, ATTN_W), lambda bi, j: (bi, j, 0)),
                  prev, cur, nxt, prev_t, cur_t, nxt_t, ctx, ctx_t, bias_first, bias_mid, bias_last],
        out_specs=pl.BlockSpec((1, tq, ATTN_W), lambda bi, j: (bi, j, 0)),
        out_shape=jax.ShapeDtypeStruct((b, l, ATTN_W), BF16),
        compiler_params=_cparams(2),
        name="window_attention",
    )(sink, q, k, k, k, v, v, v, k_ctx, v_ctx, bias, bias, bias)


def _ctx_attn_kernel(sink_ref, q_ref, k_ref, v_ref, o_ref):
    job = (0, q_ref.shape[1], [_half_variants(k_ref[0], 1)], [_half_variants(v_ref[0], 0)], [None])
    _attend([job], q_ref[0], sink_ref, o_ref)


def _context_attention(q, k, v, sink):
    b, l, _ = q.shape
    kvw = k.shape[2]
    return pl.pallas_call(
        _ctx_attn_kernel,
        grid=(b,),
        in_specs=[pl.BlockSpec(memory_space=pltpu.SMEM),
                  pl.BlockSpec((1, l, ATTN_W), lambda bi: (bi, 0, 0)),
                  pl.BlockSpec((1, l, kvw), lambda bi: (bi, 0, 0)),
                  pl.BlockSpec((1, kvw, l), lambda bi: (bi, 0, 0))],
        out_specs=pl.BlockSpec((1, l, ATTN_W), lambda bi: (bi, 0, 0)),
        out_shape=jax.ShapeDtypeStruct((b, l, ATTN_W), BF16),
        compiler_params=_cparams(1),
        name="context_attention",
    )(sink, q, k, v)


def _taps_kernel(seq_len, z_ref, w1_ref, b1_ref, f1_ref, w2_ref, b2_ref, f2_ref, w3t_ref,
                 dl_ref, o_ref):
    tn = z_ref.shape[1]
    ch = dl_ref.shape[0]
    z = z_ref[...]
    h = jnp.sin(f1_ref[...] * (jnp.dot(w1_ref[...], z, preferred_element_type=F32,
                                       precision=HIGHEST) + b1_ref[...]))
    h = jnp.sin(f2_ref[...] * (jnp.dot(w2_ref[...], h, preferred_element_type=F32,
                                       precision=HIGHEST) + b2_ref[...]))
    full = jnp.dot(w3t_ref[...], h.astype(BF16), preferred_element_type=F32)
    n0 = pl.program_id(0) * tn
    is_fwd = n0 < seq_len
    decay = jnp.exp(-z[0:1, :] * dl_ref[...])
    pos = n0 + lax.broadcasted_iota(jnp.int32, (1, tn), 1)
    for o in range(HY_ORDER):
        fwd = full[o * 2 * ch:o * 2 * ch + ch]
        bwd = full[o * 2 * ch + ch:(o + 1) * 2 * ch]
        taps = jnp.where(is_fwd, fwd, bwd) * decay
        o_ref[o] = jnp.where(pos == seq_len, 0.0, taps).astype(o_ref.dtype)


def _filter_tables(l):
    t = jnp.linspace(0.0, 1.0, l, dtype=F32)[:, None]
    w = (2.0 * math.pi / l) * jnp.arange(l, dtype=F32)[:, None]
    f = jnp.linspace(1e-4, HY_BANDS - 1, HY_BANDS, dtype=F32)[None, :]
    z = jnp.concatenate([t, jnp.cos(f * w), -jnp.sin(f * w)], axis=-1)
    zc = jnp.concatenate([z, z[-1:], jnp.flip(z[1:], axis=0)], axis=0)
    return jnp.pad(zc, ((0, 0), (0, LANES - HY_EMB))).T


def _hyena_taps(l, zct, w1t, b1, f1, w2t, b2, f2, w3t, deltas, layer):
    ch = deltas.shape[0]
    fw = w2t.shape[1]
    tn = min(1024, l)
    return pl.pallas_call(
        functools.partial(_taps_kernel, l),
        grid=(2 * l // tn,),
        in_specs=[pl.BlockSpec((LANES, tn), lambda j: (0, j)),
                  _layer_spec((fw, LANES), layer),
                  _layer_spec((fw, 1), layer),
                  _layer_spec((fw, 1), layer),
                  _layer_spec((fw, fw), layer),
                  _layer_spec((fw, 1), layer),
                  _layer_spec((fw, 1), layer),
                  _layer_spec(w3t.shape[1:], layer),
                  _full_spec((ch, 1))],
        out_specs=pl.BlockSpec((HY_ORDER, ch, tn), lambda j: (0, 0, j)),
        out_shape=jax.ShapeDtypeStruct((HY_ORDER, ch, 2 * l), BF16),
        compiler_params=_cparams(1),
        name="hyena_taps",
    )(zct, w1t, b1, f1, w2t, b2, f2, w3t, deltas)


def _dft_consts(na, nb):
    a = np.arange(na)
    fa = np.exp(-2j * np.pi * np.outer(a, a) / na)
    b = np.arange(nb)
    g = np.exp(-2j * np.pi * np.outer(b, b) / nb)
    fr, fi = fa.real, fa.imag
    gr, gi = g.real, g.imag
    h = na // 2
    m1 = np.block([[fr[:, :h], -fi[:, :h]], [fi[:, :h], fr[:, :h]]])
    m1f = np.concatenate([fr, fi], axis=0)
    m2 = np.block([[gr, gi], [-gi, gr]])
    m2i = np.block([[gr, -gi], [gi, gr]])
    m1i = np.block([[fr[:h], fi[:h]], [-fi[:h], fr[:h]]])
    tw = np.exp(-2j * np.pi * np.outer(a, b) / (na * nb))
    bf = lambda m: jnp.asarray(m, dtype=F32).astype(BF16)
    return dict(m1=bf(m1), m1f=bf(m1f), m2=bf(m2), m2i=bf(m2i), m1i=bf(m1i),
                tr=bf(tw.real), ti=bf(tw.imag))


def _stage2(a_all, tr, ti, m2):
    na, nb = tr.shape
    n = a_all.shape[1] // nb
    lhs = []
    for i in range(n):
        ar = a_all[:na, i * nb:(i + 1) * nb]
        ai = a_all[na:, i * nb:(i + 1) * nb]
        lhs.append(jnp.concatenate([ar * tr - ai * ti, ar * ti + ai * tr], axis=1))
    return jnp.dot(jnp.concatenate(lhs, axis=0), m2, preferred_element_type=F32)


def _filter_spectra(taps, m1f, m2, tr, ti):
    na, nb = tr.shape
    a_all = jnp.dot(m1f, jnp.concatenate(taps, axis=1), preferred_element_type=F32).astype(BF16)
    x = (_stage2(a_all, tr, ti, m2) * (1.0 / (na * nb))).astype(BF16)
    return [x[i * na:(i + 1) * na] for i in range(len(taps))]


def _fft_conv(zs, hfs, m1, m2, m2i, m1i, tr, ti):
    na, nb = tr.shape
    n = len(zs)
    h = na // 2
    zc = jnp.concatenate([jnp.concatenate([zr, zi], axis=0) for zr, zi in zs], axis=1)
    a_all = jnp.dot(m1, zc.astype(BF16), preferred_element_type=F32).astype(BF16)
    x_all = _stage2(a_all, tr, ti, m2).astype(BF16)
    ys = []
    for i in range(n):
        x = x_all[i * na:(i + 1) * na]
        xr, xi = x[:, :nb], x[:, nb:]
        hr, hi = hfs[i][:, :nb], hfs[i][:, nb:]
        ys.append(jnp.concatenate([xr * hr - xi * hi, xr * hi + xi * hr], axis=1))
    b_all = jnp.dot(jnp.concatenate(ys, axis=0), m2i, preferred_element_type=F32).astype(BF16)
    rhs = []
    for i in range(n):
        bm = b_all[i * na:(i + 1) * na]
        br, bi = bm[:, :nb], bm[:, nb:]
        rhs.append(jnp.concatenate([br * tr + bi * ti, bi * tr - br * ti], axis=0))
    yo = jnp.dot(m1i, jnp.concatenate(rhs, axis=1), preferred_element_type=F32)
    return [(yo[:h, i * nb:(i + 1) * nb], yo[h:, i * nb:(i + 1) * nb]) for i in range(n)]


def _hyena_kernel(hb_ref, x1_ref, x2_ref, z_ref, taps_ref,
                  m1_ref, m1f_ref, m2_ref, m2i_ref, m1i_ref, tr_ref, ti_ref, o_ref):
    nbatch, ct = x1_ref.shape[:2]
    tr, ti = tr_ref[...], ti_ref[...]
    mats = (m1_ref[...], m2_ref[...], m2i_ref[...], m1i_ref[...], tr, ti)
    gates = (x1_ref, x2_ref)
    c0 = pl.program_id(0) * ct

    def short(ref, stream, b, c):
        return ref[b, c].astype(F32)

    specs = _filter_spectra([taps_ref[o, c] for o in range(HY_ORDER) for c in range(ct)],
                            m1f_ref[...], m2_ref[...], tr, ti)
    seqs = [(c, p) for c in range(ct) for p in range(nbatch // 2)]
    zz = [(short(z_ref, 2, 2 * p, c), short(z_ref, 2, 2 * p + 1, c)) for c, p in seqs]
    for o in range(HY_ORDER):
        ys = _fft_conv(zz, [specs[o * ct + c] for c, _ in seqs], *mats)
        zz = [tuple(short(gates[o], o, 2 * p + q, c) * (ys[i][q] + hb_ref[o, c0 + c] * zz[i][q])
                    for q in range(2)) for i, (c, p) in enumerate(seqs)]
    for i, (c, p) in enumerate(seqs):
        for q in range(2):
            o_ref[2 * p + q, c] = zz[i][q].astype(o_ref.dtype)


def _hyena_long(hyt, taps, hy_bias, dc, ct):
    b, c3, l = hyt.shape
    ch = c3 // 3
    nb = LANES
    na = 2 * l // nb
    assert b % 2 == 0 and ct % 2 == 0 and ch % ct == 0
    h4 = hyt.reshape(b, c3, na // 2, nb)
    t4 = taps.reshape(HY_ORDER, ch, na, nb)
    blk = (b, ct, na // 2, nb)
    nct = ch // ct
    smem = pl.BlockSpec(memory_space=pltpu.SMEM)
    out = pl.pallas_call(
        _hyena_kernel,
        grid=(nct,),
        in_specs=[smem,
                  pl.BlockSpec(blk, lambda j: (0, j, 0, 0)),
                  pl.BlockSpec(blk, lambda j: (0, nct + j, 0, 0)),
                  pl.BlockSpec(blk, lambda j: (0, 2 * nct + j, 0, 0)),
                  pl.BlockSpec((HY_ORDER, ct, na, nb), lambda j: (0, j, 0, 0)),
                  _full_spec((2 * na, na)), _full_spec((2 * na, na)),
                  _full_spec((2 * nb, 2 * nb)), _full_spec((2 * nb, 2 * nb)),
                  _full_spec((na, 2 * na)), _full_spec((na, nb)), _full_spec((na, nb))],
        out_specs=pl.BlockSpec(blk, lambda j: (0, j, 0, 0)),
        out_shape=jax.ShapeDtypeStruct((b, ch, na // 2, nb), BF16),
        compiler_params=_cparams(1),
        name="hyena_long",
    )(hy_bias, h4, h4, h4, t4,
      dc["m1"], dc["m1f"], dc["m2"], dc["m2i"], dc["m1i"], dc["tr"], dc["ti"])
    return out.reshape(b, ch, l)


def _dense_dft_consts(l):
    n = 2 * l
    k = np.arange(n)
    ang = 2 * np.pi * np.outer(k, k) / n
    cos, sin = np.cos(ang), np.sin(ang)
    fwd = np.concatenate([cos, -sin], axis=1)
    inv = np.concatenate([cos[:, :l], -sin[:, :l]], axis=0) / n
    bf = lambda m: jnp.asarray(m, dtype=F32).astype(BF16)
    return dict(fwd_half=bf(fwd[:l]), fwd=bf(fwd), inv=bf(inv))


def _hyena_short_kernel(hb_ref, x1_ref, x2_ref, z_ref, taps_ref, fh_ref, ff_ref, inv_ref, o_ref):
    nbatch, ct, l = x1_ref.shape
    n = 2 * l

    def short(ref, stream, b):
        return ref[b].astype(F32)

    gates = (x1_ref, x2_ref)
    zz = [short(z_ref, 2, b) for b in range(nbatch)]
    for o in range(HY_ORDER):
        hf = jnp.dot(taps_ref[o].astype(BF16), ff_ref[...], preferred_element_type=F32)
        hr, hi = hf[:, :n], hf[:, n:]
        x_all = jnp.dot(jnp.concatenate(zz, axis=0).astype(BF16), fh_ref[...],
                        preferred_element_type=F32)
        ys = []
        for b in range(nbatch):
            x = x_all[b * ct:(b + 1) * ct]
            xr, xi = x[:, :n], x[:, n:]
            ys.append(jnp.concatenate([xr * hr - xi * hi, xr * hi + xi * hr], axis=1).astype(BF16))
        conv = jnp.dot(jnp.concatenate(ys, axis=0), inv_ref[...], preferred_element_type=F32)
        zz = [short(gates[o], o, b) * (conv[b * ct:(b + 1) * ct] + hb_ref[o] * zz[b])
              for b in range(nbatch)]
    for b in range(nbatch):
        o_ref[b] = zz[b].astype(o_ref.dtype)


def _hyena_short(hyt, taps, hy_bias3, dd, ct):
    b, c3, l = hyt.shape
    ch = c3 // 3
    n = 2 * l
    nct = ch // ct
    blk = (b, ct, l)
    return pl.pallas_call(
        _hyena_short_kernel,
        grid=(nct,),
        in_specs=[pl.BlockSpec((HY_ORDER, ct, 1), lambda j: (0, j, 0)),
                  pl.BlockSpec(blk, lambda j: (0, j, 0)),
                  pl.BlockSpec(blk, lambda j: (0, nct + j, 0)),
                  pl.BlockSpec(blk, lambda j: (0, 2 * nct + j, 0)),
                  pl.BlockSpec((HY_ORDER, ct, n), lambda j: (0, j, 0)),
                  _full_spec((l, 2 * n)), _full_spec((n, 2 * n)), _full_spec((2 * n, l))],
        out_specs=pl.BlockSpec(blk, lambda j: (0, j, 0)),
        out_shape=jax.ShapeDtypeStruct((b, ch, l), BF16),
        compiler_params=_cparams(1),
        name="hyena_short",
    )(hy_bias3, hyt, hyt, hyt, taps, dd["fwd_half"], dd["fwd"], dd["inv"])


def _mixout_kernel(a_ref, hy_ref, up_ref, uc_ref, un_ref, x_ref, m_ref, g_ref, wa_ref, wh_ref, wc_ref,
                   dw_ref, db_ref, lg_ref, lb_ref, o_ref, buf, cv_buf):
    tm = uc_ref.shape[1]
    ch = cv_buf.shape[1]
    j = pl.program_id(1)
    nj = pl.num_programs(1)
    y = jnp.dot(a_ref[0], wa_ref[...], preferred_element_type=F32)
    y = y + pl.dot(hy_ref[0], wh_ref[...], trans_a=True)

    def glu(u):
        return u.astype(F32)

    buf[0, 0:HALO] = jnp.where(j > 0, glu(up_ref[0]), 0.0)
    buf[0, HALO:HALO + tm] = glu(uc_ref[0])
    buf[0, HALO + tm:] = jnp.where(j < nj - 1, glu(un_ref[0]), 0.0)
    span = tm + 2 * HALO - SUBLANES
    for s in range(1, SUBLANES):
        buf[s, 0:span] = buf[0, pl.ds(s, span), :]
    pad = (CV_K - 1) // 2
    rc = 128
    for r0 in range(0, tm, rc):
        acc = jnp.zeros((rc, ch), F32) + db_ref[...]
        for k in range(CV_K):
            off = HALO - pad + k
            acc = acc + dw_ref[k:k + 1, :] * buf[off % SUBLANES, pl.ds(r0 + off - off % SUBLANES, rc), :]
        mu = jnp.mean(acc, axis=-1, keepdims=True)
        cen = acc - mu
        var = jnp.mean(cen * cen, axis=-1, keepdims=True)
        cv = cen * lax.rsqrt(var + EPS) * lg_ref[...] + lb_ref[...]
        cv_buf[r0:r0 + rc] = _silu(cv).astype(cv_buf.dtype)
    y = y + jnp.dot(cv_buf[...], wc_ref[...], preferred_element_type=F32)
    o_ref[0] = x_ref[0] + _rms(y, m_ref[0, 2:3, :] * g_ref[...])


def _mixout(attn, hyt, u, x, mods, mod_row, g_post, wa, wh, wc, dw_w, dw_b, ln_g, ln_b, layer, tm):
    b, l, d = x.shape
    c2 = u.shape[2]
    ch = c2
    r = tm // HALO
    nh = l // HALO
    mrow = (lambda bi: bi) if mod_row is None else (lambda bi: mod_row)
    return pl.pallas_call(
        _mixout_kernel,
        grid=(b, l // tm),
        in_specs=[pl.BlockSpec((1, tm, attn.shape[2]), lambda bi, j: (bi, j, 0)),
                  pl.BlockSpec((1, hyt.shape[1], tm), lambda bi, j: (bi, 0, j)),
                  pl.BlockSpec((1, HALO, c2), lambda bi, j: (bi, jnp.maximum(j * r - 1, 0), 0)),
                  pl.BlockSpec((1, tm, c2), lambda bi, j: (bi, j, 0)),
                  pl.BlockSpec((1, HALO, c2), lambda bi, j: (bi, jnp.minimum((j + 1) * r, nh - 1), 0)),
                  pl.BlockSpec((1, tm, d), lambda bi, j: (bi, j, 0)),
                  pl.BlockSpec((1, 6, d), lambda bi, j: (mrow(bi), 0, 0)),
                  _layer_spec((1, d), layer),
                  _layer_spec(wa.shape[1:], layer),
                  _layer_spec(wh.shape[1:], layer),
                  _layer_spec(wc.shape[1:], layer),
                  _layer_spec((CV_K, ch), layer),
                  _layer_spec((1, ch), layer), _layer_spec((1, ch), layer), _layer_spec((1, ch), layer)],
        out_specs=pl.BlockSpec((1, tm, d), lambda bi, j: (bi, j, 0)),
        out_shape=jax.ShapeDtypeStruct((b, l, d), F32),
        scratch_shapes=[pltpu.VMEM((SUBLANES, tm + 2 * HALO, ch), F32), pltpu.VMEM((tm, ch), BF16)],
        compiler_params=_cparams(2),
        name="mix_outproj",
    )(attn, hyt, u, u, u, x, mods, g_post, wa, wh, wc, dw_w, dw_b, ln_g, ln_b)


def _ffn_kernel(n_chunks, xp_ref, xc_ref, xn_ref, m_ref, gpre_ref, gpost_ref, wu_ref, wg_ref,
                dw_ref, db_ref, wd_ref, o_ref):
    tm = xc_ref.shape[1]
    dff = wu_ref.shape[1]
    cw = dff // n_chunks
    j = pl.program_id(1)
    nj = pl.num_programs(1)
    shift, scale, gate = m_ref[0, 3:4, :], m_ref[0, 4:5, :], m_ref[0, 5:6, :]
    gpre = gpre_ref[...]

    def prep(x):
        return (_rms(x, gpre) * (1.0 + scale) + shift).astype(BF16)

    xc = xc_ref[0]
    hb = prep(xc)
    hext = jnp.concatenate([prep(xp_ref[0]), hb, prep(xn_ref[0])], axis=0)
    y = jnp.zeros((tm, o_ref.shape[2]), F32)
    for c in range(n_chunks):
        cols = slice(c * cw, (c + 1) * cw)
        g = jnp.dot(hext, wg_ref[:, cols], preferred_element_type=F32)
        u = jnp.dot(hb, wu_ref[:, cols], preferred_element_type=F32)
        g = jnp.concatenate([jnp.where(j > 0, g[:HALO], 0.0), g[HALO:HALO + tm],
                             jnp.where(j < nj - 1, g[HALO + tm:], 0.0)], axis=0)
        rows = tm + 2 * HALO
        g_prev = pltpu.roll(g, 1, axis=0)[HALO:HALO + tm]
        g_next = pltpu.roll(g, rows - 1, axis=0)[HALO:HALO + tm]
        conv = (dw_ref[0:1, cols] * g_prev + dw_ref[1:2, cols] * g[HALO:HALO + tm]
                + dw_ref[2:3, cols] * g_next + db_ref[:, cols])
        act = (_silu(conv) * u).astype(BF16)
        y = y + jnp.dot(act, wd_ref[cols, :], preferred_element_type=F32)
    o_ref[0] = xc + _rms(y, gate * gpost_ref[...])


def _ffn(x, mods, mod_row, g_pre, g_post, wu, wg, dw_w, dw_b, wd, layer, tm, n_chunks):
    b, l, d = x.shape
    dff = wu.shape[2]
    r = tm // HALO
    nh = l // HALO
    mrow = (lambda bi: bi) if mod_row is None else (lambda bi: mod_row)
    return pl.pallas_call(
        functools.partial(_ffn_kernel, n_chunks),
        grid=(b, l // tm),
        in_specs=[pl.BlockSpec((1, HALO, d), lambda bi, j: (bi, jnp.maximum(j * r - 1, 0), 0)),
                  pl.BlockSpec((1, tm, d), lambda bi, j: (bi, j, 0)),
                  pl.BlockSpec((1, HALO, d), lambda bi, j: (bi, jnp.minimum((j + 1) * r, nh - 1), 0)),
                  pl.BlockSpec((1, 6, d), lambda bi, j: (mrow(bi), 0, 0)),
                  _layer_spec((1, d), layer), _layer_spec((1, d), layer),
                  _layer_spec((d, dff), layer), _layer_spec((d, dff), layer),
                  _layer_spec((FFN_K, dff), layer), _layer_spec((1, dff), layer),
                  _layer_spec((dff, d), layer)],
        out_specs=pl.BlockSpec((1, tm, d), lambda bi, j: (bi, j, 0)),
        out_shape=jax.ShapeDtypeStruct((b, l, d), F32),
        compiler_params=_cparams(2),
        name="conv_ffn",
    )(x, x, x, mods, g_pre, g_post, wu, wg, dw_w, dw_b, wd)


def _rope_tables(l):
    rows = l // GRID_W
    row = jnp.repeat(jnp.arange(rows, dtype=F32), GRID_W)
    col = jnp.tile(jnp.arange(GRID_W, dtype=F32), rows)
    inv = ROPE_THETA ** (-jnp.arange(0, AXIS_ROT, 2, dtype=F32) / AXIS_ROT)
    ar, ac = row[:, None] * inv, col[:, None] * inv
    cos = jnp.concatenate([jnp.cos(ar), jnp.cos(ar), jnp.cos(ac), jnp.cos(ac)], axis=1)
    sin = jnp.concatenate([-jnp.sin(ar), jnp.sin(ar), -jnp.sin(ac), jnp.sin(ac)], axis=1)
    rep = LANES // HEAD_DIM
    return jnp.tile(cos, (1, rep)), jnp.tile(sin, (1, rep))


def kernel(x, c, ctx, c_ctx, w_mod, b_mod, g_pre_mix, g_post_mix, g_pre_ffn, g_post_ffn, w_in, attn_sink, hy_short_w, hy_short_b, hy_w1, hy_b1, hy_freq1, hy_w2, hy_b2, hy_freq2, hy_w3, hy_bias, cv_dw_w, cv_dw_b, cv_ln_g, cv_ln_b, w_out, w_up, ffn_dw_w, ffn_dw_b, w_down):
    b, l, d = x.shape
    lc = ctx.shape[1]
    depth = w_in.shape[0]
    hy_ch = hy_bias.shape[2]
    cv_ch = cv_dw_w.shape[2]
    dff = w_down.shape[1]
    k0 = ATTN_W
    hy0 = k0 + 2 * KV_W
    cv0 = hy0 + (HY_ORDER + 1) * hy_ch
    fw = hy_w2.shape[1]

    v0 = k0 + KV_W
    wqkv = w_in[:, :, :k0].astype(BF16)
    whyt = jnp.swapaxes(w_in[:, :, k0:cv0], 1, 2).astype(BF16)
    wcv = w_in[:, :, cv0:].astype(BF16)
    wo_a = w_out[:, :ATTN_W].astype(BF16)
    wo_h = w_out[:, ATTN_W:ATTN_W + hy_ch].astype(BF16)
    wo_c = w_out[:, ATTN_W + hy_ch:].astype(BF16)
    wu = w_up[:, :, :dff].astype(BF16)
    wg = w_up[:, :, dff:].astype(BF16)
    wd = w_down.astype(BF16)
    row = lambda a: a[:, None, :]
    gpm, gqm, gpf, gqf = row(g_pre_mix), row(g_post_mix), row(g_pre_ffn), row(g_post_ffn)
    col = lambda a: a[:, :, None]
    w1t = jnp.swapaxes(jnp.pad(hy_w1, ((0, 0), (0, LANES - HY_EMB), (0, 0))), 1, 2)
    w3t = jnp.swapaxes(hy_w3, 1, 2).astype(BF16)
    deltas = jnp.abs(jnp.linspace(math.log(HY_TARGET) / HY_SLOW, math.log(HY_TARGET) / HY_FAST,
                                  hy_ch, dtype=F32))[:, None]
    swc = hy_short_w[..., None]
    sbc = hy_short_b[..., None]
    hb4 = hy_bias[..., None]

    n_rows = 8 * ((b + 1 + 7) // 8)
    cond = jnp.zeros((n_rows, d), F32).at[:b].set(c).at[b].set(c_ctx)
    mods = _modulation(cond, w_mod, b_mod).reshape(depth, n_rows, 6, d)

    cs, sn = _rope_tables(l)
    tab_l, tab_c = _filter_tables(l), _filter_tables(lc)
    dc = _dft_consts(2 * l // LANES, LANES)
    dd = _dense_dft_consts(lc)
    tm_in, tm_mix, tm_ffn, tq, hy_ct = min(1024, l), min(1024, l), min(512, l), min(2048, l), 16
    tmc = lc

    for i in range(depth):
        last = i == depth - 1
        m_i = mods[i]
        hyp = (w1t, col(hy_b1), col(hy_freq1), jnp.swapaxes(hy_w2, 1, 2), col(hy_b2), col(hy_freq2),
               w3t, deltas)

        q, k, v, hyt, cvu = _inproj(x, m_i, None, gpm, wqkv, whyt, wcv, swc, sbc, i, cs, sn, tm_in)
        qc, kc, vc, hyt_c, cvu_c = _inproj(ctx, m_i, b, gpm, wqkv, whyt, wcv, swc, sbc, i, None, None, tmc)

        attn = _window_attention(q, k, v, kc, vc, attn_sink[i][None, :], tq)
        taps = _hyena_taps(l, tab_l, *hyp, i)
        hy = _hyena_long(hyt, taps, hy_bias[i], dc, hy_ct)
        cvp = (cv_dw_w, row(cv_dw_b), row(cv_ln_g), row(cv_ln_b))
        x = _mixout(attn, hy, cvu, x, m_i, None, gqm, wo_a, wo_h, wo_c, *cvp, i, tm_mix)
        x = _ffn(x, m_i, None, gpf, gqf, wu, wg, ffn_dw_w, row(ffn_dw_b), wd, i, tm_ffn, 1)

        if not last:
            attn_c = _context_attention(qc, kc, vc, attn_sink[i][None, :])
            taps_c = _hyena_taps(lc, tab_c, *hyp, i)
            hy_c = _hyena_short(hyt_c, taps_c, hb4[i], dd, 32)
            ctx = _mixout(attn_c, hy_c, cvu_c, ctx, m_i, b, gqm, wo_a, wo_h, wo_c, *cvp, i, tmc)
            ctx = _ffn(ctx, m_i, b, gpf, gqf, wu, wg, ffn_dw_w, row(ffn_dw_b), wd, i, tmc, 1)
    return x
```
